```python
import math
import jax, jax.numpy as jnp
from jax import lax
import numpy as np

D_MODEL = 2048
BATCH = 4
SEQ = 2048
DEPTH = 2
DEC_BATCH = 8
DEC_SEQ = 4
PAST_LEN = 16384
PAGE_SIZE = 128

POOL_WIDTH = D_MODEL // 2
POOL_WINDOWS = (2, 4, 8, 16)
N_POOL_GROUPS = len(POOL_WINDOWS)
POOL_GROUP = POOL_WIDTH // N_POOL_GROUPS
POOL_BUF = max(POOL_WINDOWS) - 1
ATTN_WIDTH = D_MODEL // 2
N_HEADS = 8
HEAD_DIM = ATTN_WIDTH // N_HEADS
N_IDX_HEADS = 16
IDX_DIM = 64
IDX_SCALE = (N_IDX_HEADS * IDX_DIM) ** -0.5
TOPK_MAX = 256
Q_BLOCK = 128
ROPE_THETA = 10000.0
LN_EPS = 1e-5
ALPHA = (2 * DEPTH) ** 0.25
BETA = (8 * DEPTH) ** -0.25
IN_SIZES = (POOL_WIDTH, POOL_WIDTH, ATTN_WIDTH, ATTN_WIDTH, ATTN_WIDTH, ATTN_WIDTH,
            N_IDX_HEADS * IDX_DIM, IDX_DIM, N_IDX_HEADS, D_MODEL, D_MODEL)
IN_WIDTH = sum(IN_SIZES)

kernel_name = 'hybrid_pool_dsa_decoder_step'


def layer_norm(x, g, b):
    xf = x.astype(jnp.float32)
    mu = jnp.mean(xf, axis=-1, keepdims=True)
    var = jnp.mean(jnp.square(xf - mu), axis=-1, keepdims=True)
    out = (xf - mu) * lax.rsqrt(var + LN_EPS) * g.astype(jnp.float32) + b.astype(jnp.float32)
    return out.astype(x.dtype)


def rope(x, pos):
    half = x.shape[-1] // 2
    inv = ROPE_THETA ** (-jnp.arange(half, dtype=jnp.float32) / half)
    ang = pos.astype(jnp.float32)[:, None] * inv[None, :]
    cos = jnp.cos(ang)[:, None, :]
    sin = jnp.sin(ang)[:, None, :]
    xf = x.astype(jnp.float32)
    x1, x2 = xf[..., :half], xf[..., half:]
    return jnp.concatenate([x1 * cos - x2 * sin, x2 * cos + x1 * sin], axis=-1).astype(x.dtype)


def project(x, w_in):
    h = jnp.einsum('btd,dn->btn', x, w_in)
    cuts, c = [], 0
    for s in IN_SIZES[:-1]:
        c += s
        cuts.append(c)
    return jnp.split(h, cuts, axis=-1)


def make_heads(q, k, v, iq, ik, pos):
    B, T = q.shape[:2]
    q = rope(q.reshape(B, T, N_HEADS, HEAD_DIM), pos)
    k = rope(k.reshape(B, T, N_HEADS, HEAD_DIM), pos)
    v = v.reshape(B, T, N_HEADS, HEAD_DIM)
    iq = rope(iq.reshape(B, T, N_IDX_HEADS, IDX_DIM), pos)
    ik = rope(ik[:, :, None, :], pos)[:, :, 0]
    return q, k, v, iq, ik


def pool_mixer(ext, pos, w_mix, scale):
    B = ext.shape[0]
    T = pos.shape[0]
    ef = ext.astype(jnp.float32)
    cs = jnp.concatenate([jnp.zeros((B, 1, POOL_WIDTH), jnp.float32), jnp.cumsum(ef, axis=1)], axis=1)
    end = cs[:, POOL_BUF + 1:]
    means = []
    for g, w in enumerate(POOL_WINDOWS):
        c0, c1 = g * POOL_GROUP, (g + 1) * POOL_GROUP
        start = cs[:, POOL_BUF + 1 - w: POOL_BUF + 1 - w + T, c0:c1]
        cnt = jnp.minimum(pos + 1, w).astype(jnp.float32)[None, :, None]
        means.append((end[:, :, c0:c1] - start) / cnt)
    d = (jnp.concatenate(means, axis=-1) - ef[:, POOL_BUF:]).astype(ext.dtype)
    d = d.reshape(B, T, N_POOL_GROUPS, POOL_GROUP)
    mixed = jnp.einsum('btgc,gce->btge', d, w_mix).reshape(B, T, POOL_WIDTH)
    return mixed * scale


def indexer_scores(iq, ik, iw):
    logits = jnp.einsum('...qhi,...si->...qhs', iq.astype(jnp.float32), ik.astype(jnp.float32))
    return jnp.einsum('...qhs,...qh->...qs', jax.nn.relu(logits), iw.astype(jnp.float32) * IDX_SCALE)


def attend(q, kg, vg, valid):
    s = jnp.einsum('...hd,...khd->...hk', q.astype(jnp.float32), kg.astype(jnp.float32)) * (HEAD_DIM ** -0.5)
    s = jnp.where(valid[..., None, :], s, -jnp.inf)
    p = jax.nn.softmax(s, axis=-1)
    return jnp.einsum('...hk,...khd->...hd', p.astype(vg.dtype), vg)


def prompt_sparse_attention(q, k, v, iq, ik, iw):
    B, T = q.shape[:2]
    nb = T // Q_BLOCK
    k_top = min(TOPK_MAX, T // 4)
    key_pos = jnp.arange(T, dtype=jnp.int32)

    def blk(a):
        return a.reshape((B * nb, Q_BLOCK) + a.shape[2:])

    b_idx = jnp.repeat(jnp.arange(B, dtype=jnp.int32), nb)
    t0 = jnp.tile(jnp.arange(nb, dtype=jnp.int32) * Q_BLOCK, B)

    def one(args):
        qb, iqb, iwb, b, s0 = args
        qpos = s0 + jnp.arange(Q_BLOCK, dtype=jnp.int32)
        sc = indexer_scores(iqb, ik[b], iwb)
        sc = jnp.where(key_pos[None, :] <= qpos[:, None], sc, -jnp.inf)
        _, idx = lax.top_k(sc, k_top)
        valid = idx <= qpos[:, None]
        return attend(qb, k[b, idx], v[b, idx], valid)

    out = lax.map(one, (blk(q), blk(iq), blk(iw), b_idx, t0))
    return out.reshape(B, T, ATTN_WIDTH)


def sample_sparse_attention(q, k_new, v_new, iq, ik_new, iw, cache_k, cache_v, cache_ik, page_table):
    DB, n = q.shape[:2]
    n_pages = page_table.shape[1]
    past = n_pages * PAGE_SIZE
    L = past + n
    k_top = min(TOPK_MAX, L // 4)
    ik_past = cache_ik[page_table].reshape(DB, past, IDX_DIM).astype(ik_new.dtype)
    ik_all = jnp.concatenate([ik_past, ik_new], axis=1)
    qpos = past + jnp.arange(n, dtype=jnp.int32)
    sc = indexer_scores(iq, ik_all, iw)
    sc = jnp.where(jnp.arange(L, dtype=jnp.int32)[None, None, :] <= qpos[None, :, None], sc, -jnp.inf)
    _, idx = lax.top_k(sc, k_top)
    valid = idx <= qpos[None, :, None]
    bi = jnp.arange(DB, dtype=jnp.int32)[:, None, None]
    pidx = jnp.minimum(idx, past - 1)
    phys = page_table[bi, pidx // PAGE_SIZE]
    off = pidx % PAGE_SIZE
    nidx = jnp.clip(idx - past, 0, n - 1)
    is_past = (idx < past)[..., None, None]
    kg = jnp.where(is_past, cache_k[phys, off].astype(k_new.dtype), k_new[bi, nidx])
    vg = jnp.where(is_past, cache_v[phys, off].astype(v_new.dtype), v_new[bi, nidx])
    return attend(q, kg, vg, valid).reshape(DB, n, ATTN_WIDTH)


def merge(x, pool_o, attn_o, zp, za, gp, ga, w_pool_out, w_attn_out, w_o, g, b):
    p = jnp.einsum('btc,cd->btd', pool_o * jax.nn.silu(zp), w_pool_out)
    a = jnp.einsum('btc,cd->btd', attn_o * jax.nn.silu(za), w_attn_out)
    m = jax.nn.sigmoid(gp) * p + jax.nn.sigmoid(ga) * a
    y = jnp.einsum('btd,de->bte', m, w_o)
    return layer_norm(ALPHA * x + y, g, b)


def setup_inputs(seed: int = 0) -> dict:
    key = jax.random.key(seed)
    ks = jax.random.split(key, 20)
    f32 = jnp.float32
    n_pages = PAST_LEN // PAGE_SIZE
    n_pool = (DEC_BATCH * n_pages * 5) // 4
    nrm = jax.random.normal
    perm = jax.random.permutation(ks[6], n_pool)
    page_table = perm[:DEC_BATCH * n_pages].reshape(DEC_BATCH, n_pages).astype(jnp.int32)
    return {
        'x_prompt': nrm(ks[0], (BATCH, SEQ, D_MODEL), f32),
        'x_sample': nrm(ks[1], (DEC_BATCH, DEC_SEQ, D_MODEL), f32),
        'cache_k': nrm(ks[2], (DEPTH, n_pool, PAGE_SIZE, N_HEADS, HEAD_DIM), f32),
        'cache_v': nrm(ks[3], (DEPTH, n_pool, PAGE_SIZE, N_HEADS, HEAD_DIM), f32),
        'cache_idx_k': nrm(ks[4], (DEPTH, n_pool, PAGE_SIZE, IDX_DIM), f32),
        'state_pool': nrm(ks[5], (DEPTH, DEC_BATCH, POOL_BUF, POOL_WIDTH), f32),
        'page_table': page_table,
        'ln_emb_g': 1.0 + 0.05 * nrm(ks[7], (D_MODEL,), f32),
        'ln_emb_b': 0.02 * nrm(ks[8], (D_MODEL,), f32),
        'w_in': nrm(ks[9], (DEPTH, D_MODEL, IN_WIDTH), f32) * D_MODEL ** -0.5,
        'w_pool_mix': nrm(ks[10], (DEPTH, N_POOL_GROUPS, POOL_GROUP, POOL_GROUP), f32) * POOL_GROUP ** -0.5,
        'pool_scale': 1.0 + 0.1 * nrm(ks[11], (DEPTH, POOL_WIDTH), f32),
        'w_pool_out': nrm(ks[12], (DEPTH, POOL_WIDTH, D_MODEL), f32) * (POOL_WIDTH ** -0.5 * BETA),
        'w_attn_out': nrm(ks[13], (DEPTH, ATTN_WIDTH, D_MODEL), f32) * (ATTN_WIDTH ** -0.5 * BETA),
        'w_o': nrm(ks[14], (DEPTH, D_MODEL, D_MODEL), f32) * (D_MODEL ** -0.5 * BETA),
        'ln_g': 1.0 + 0.05 * nrm(ks[15], (DEPTH, D_MODEL), f32),
        'ln_b': 0.02 * nrm(ks[16], (DEPTH, D_MODEL), f32),
    }


def reference(x_prompt, x_sample, cache_k, cache_v, cache_idx_k, state_pool, page_table,
              ln_emb_g, ln_emb_b, w_in, w_pool_mix, pool_scale, w_pool_out, w_attn_out,
              w_o, ln_g, ln_b):
    B, T = x_prompt.shape[:2]
    DB, n = x_sample.shape[:2]
    n_pages = page_table.shape[1]
    pos_p = jnp.arange(T, dtype=jnp.int32)
    pos_s = n_pages * PAGE_SIZE + jnp.arange(n, dtype=jnp.int32)
    xp = layer_norm(x_prompt, ln_emb_g, ln_emb_b)
    xs = layer_norm(x_sample, ln_emb_g, ln_emb_b)
    kp, vp, ikp, plp = [], [], [], []
    kss, vss, iks, pls = [], [], [], []
    for l in range(DEPTH):
        up, zp, q, k, v, za, iq, ik, iw, gp, ga = project(xp, w_in[l])
        q, k, v, iq, ik = make_heads(q, k, v, iq, ik, pos_p)
        ext = jnp.concatenate([jnp.zeros((B, POOL_BUF, POOL_WIDTH), up.dtype), up], axis=1)
        pool_o = pool_mixer(ext, pos_p, w_pool_mix[l], pool_scale[l])
        attn_o = prompt_sparse_attention(q, k, v, iq, ik, iw)
        kp.append(k)
        vp.append(v)
        ikp.append(ik)
        plp.append(ext[:, -POOL_BUF:])
        xp = merge(xp, pool_o, attn_o, zp, za, gp, ga, w_pool_out[l], w_attn_out[l], w_o[l], ln_g[l], ln_b[l])
        us, zps, qs, ks_, vs, zas, iqs, iks_, iws, gps, gas = project(xs, w_in[l])
        qs, ks_, vs, iqs, iks_ = make_heads(qs, ks_, vs, iqs, iks_, pos_s)
        ext_s = jnp.concatenate([state_pool[l].astype(us.dtype), us], axis=1)
        pool_os = pool_mixer(ext_s, pos_s, w_pool_mix[l], pool_scale[l])
        attn_os = sample_sparse_attention(qs, ks_, vs, iqs, iks_, iws, cache_k[l], cache_v[l],
                                          cache_idx_k[l], page_table)
        kss.append(ks_)
        vss.append(vs)
        iks.append(iks_)
        pls.append(ext_s[:, -POOL_BUF:])
        xs = merge(xs, pool_os, attn_os, zps, zas, gps, gas, w_pool_out[l], w_attn_out[l], w_o[l], ln_g[l], ln_b[l])
    return (xp, xs, jnp.stack(kp), jnp.stack(vp), jnp.stack(ikp), jnp.stack(plp),
            jnp.stack(kss), jnp.stack(vss), jnp.stack(iks), jnp.stack(pls))
```

```python
import functools

import jax
import jax.numpy as jnp
from jax import lax
from jax.experimental import pallas as pl
from jax.experimental.pallas import tpu as pltpu

F32 = jnp.float32
BF16 = jnp.bfloat16
I32 = jnp.int32

N_HEADS = 8
HEAD_DIM = 128
N_IDX_HEADS = 16
IDX_DIM = 64
POOL_WINDOWS = (2, 4, 8, 16)
POOL_GROUP = 256
POOL_BUF = 15
TOPK_MAX = 256
Q_BLOCK = 128
ROPE_THETA = 10000.0
LN_EPS = 1e-5
PAGE_SIZE = 128

LANES = 128
SUBLANES = 8
HIST_ROWS = 16
SAMPLE_ROWS = 16
SMALL_W = 256
PAGES_PER_STEP = 8

W_SEG = 1024
SEG_GP, SEG_GA = 0, 1
SEG_UP, SEG_ZP, SEG_Q, SEG_K, SEG_V, SEG_ZA, SEG_IQ = 4, 5, 6, 7, 8, 9, 10
SEG_SMALL = 44
H_WIDTH = 11 * W_SEG + SMALL_W

INT_MIN = -2 ** 31
NT_DIMS = (((1,), (1,)), ((), ()))
MASK_NEG = float("-inf")
M_INIT = -1e30


def _cparams(sem, vmem_mb):
    return pltpu.CompilerParams(dimension_semantics=sem, vmem_limit_bytes=vmem_mb * 1024 * 1024)


def _silu(z):
    return z * jax.nn.sigmoid(z)


def _layer_norm_rows(x, g, b):
    mu = jnp.mean(x, axis=-1, keepdims=True)
    xc = x - mu
    var = jnp.mean(xc * xc, axis=-1, keepdims=True)
    return xc * lax.rsqrt(var + LN_EPS) * g + b


def _ln_kernel(x_ref, g_ref, b_ref, o_ref):
    o_ref[...] = _layer_norm_rows(x_ref[...], g_ref[...], b_ref[...])


def _layer_norm(x, g, b, tm):
    m, d = x.shape
    return pl.pallas_call(
        _ln_kernel,
        out_shape=jax.ShapeDtypeStruct((m, d), F32),
        grid=(m // tm,),
        in_specs=[pl.BlockSpec((tm, d), lambda i: (i, 0)),
                  pl.BlockSpec((1, d), lambda i: (0, 0)),
                  pl.BlockSpec((1, d), lambda i: (0, 0))],
        out_specs=pl.BlockSpec((tm, d), lambda i: (i, 0)),
        compiler_params=_cparams(("parallel",), 32),
        name="ln_embed",
    )(x, g.reshape(1, d), b.reshape(1, d))


def _proj_kernel(x_ref, w_ref, o_ref, xb_ref):
    @pl.when(pl.program_id(1) == 0)
    def _():
        xb_ref[...] = x_ref[...].astype(BF16)

    o_ref[...] = jnp.dot(xb_ref[...], w_ref[...], preferred_element_type=F32)


def _project(x, w, tm, tn):
    m, d = x.shape
    n = w.shape[1]
    return pl.pallas_call(
        _proj_kernel,
        out_shape=jax.ShapeDtypeStruct((m, n), F32),
        grid=(m // tm, n // tn),
        in_specs=[pl.BlockSpec((tm, d), lambda i, j: (i, 0)),
                  pl.BlockSpec((d, tn), lambda i, j: (0, j))],
        out_specs=pl.BlockSpec((tm, tn), lambda i, j: (i, j)),
        scratch_shapes=[pltpu.VMEM((tm, d), BF16)],
        compiler_params=_cparams(("parallel", "arbitrary"), 48),
        name="in_proj",
    )(x, w)


def _rope_kernel(q_ref, k_ref, v_ref, iq_ref, sm_ref, tab_ref,
                 ko_ref, vo_ref, qb_ref, kb_ref, vb_ref, iqb_ref, iks_ref, ike_ref, iko_ref):
    cos_h, sin_h = tab_ref[0], tab_ref[1]
    cos_i, sin_i = tab_ref[2], tab_ref[3]
    cos_s, sin_s = tab_ref[4], tab_ref[5]
    lane = lax.broadcasted_iota(I32, cos_h.shape, 1)
    low_half = (lane % IDX_DIM) < (IDX_DIM // 2)

    def rope_idx(x, cos, sin):
        partner = jnp.where(low_half, pltpu.roll(x, LANES - IDX_DIM // 2, 1),
                            pltpu.roll(x, IDX_DIM // 2, 1))
        return x * cos + partner * sin

    for h in range(N_HEADS):
        sl = slice(h * HEAD_DIM, (h + 1) * HEAD_DIM)
        x = q_ref[:, sl]
        qb_ref[:, sl] = (x * cos_h + pltpu.roll(x, HEAD_DIM // 2, 1) * sin_h).astype(BF16)
        x = k_ref[:, sl]
        kr = x * cos_h + pltpu.roll(x, HEAD_DIM // 2, 1) * sin_h
        ko_ref[:, sl] = kr
        kb_ref[:, sl] = kr.astype(BF16)
        iqb_ref[:, sl] = rope_idx(iq_ref[:, sl], cos_i, sin_i).astype(BF16)
    v = v_ref[...]
    vo_ref[...] = v
    vb_ref[...] = v.astype(BF16)
    xs = rope_idx(sm_ref[:, 0:LANES], cos_s, sin_s)
    iks_ref[...] = xs
    ik_only = jnp.where(lane < IDX_DIM, xs, 0.0)
    ike_ref[...] = ik_only.astype(BF16)
    iko_ref[...] = pltpu.roll(ik_only, IDX_DIM, 1).astype(BF16)


def _rope(h, tab, tm, tab_blocks):
    m = h.shape[0]
    seg = lambda s: pl.BlockSpec((tm, W_SEG), lambda i, s=s: (i, s))
    row = lambda w: pl.BlockSpec((tm, w), lambda i: (i, 0))
    f32o = lambda w: jax.ShapeDtypeStruct((m, w), F32)
    b16o = lambda w: jax.ShapeDtypeStruct((m, w), BF16)
    return pl.pallas_call(
        _rope_kernel,
        out_shape=(f32o(W_SEG), f32o(W_SEG), b16o(W_SEG), b16o(W_SEG), b16o(W_SEG), b16o(W_SEG),
                   f32o(LANES), b16o(LANES), b16o(LANES)),
        grid=(m // tm,),
        in_specs=[seg(SEG_Q), seg(SEG_K), seg(SEG_V), seg(SEG_IQ),
                  pl.BlockSpec((tm, SMALL_W), lambda i: (i, SEG_SMALL)),
                  pl.BlockSpec((6, tm, LANES), lambda i: (0, i % tab_blocks, 0))],
        out_specs=(row(W_SEG), row(W_SEG), row(W_SEG), row(W_SEG), row(W_SEG), row(W_SEG),
                   row(LANES), row(LANES), row(LANES)),
        compiler_params=_cparams(("parallel",), 48),
        name="rope_split",
    )(h, h, h, h, h, tab)


def _rope_tables(pos):
    def cs(half):
        inv = ROPE_THETA ** (-jnp.arange(half, dtype=F32) / half)
        ang = pos.astype(F32)[:, None] * inv[None, :]
        return jnp.cos(ang), jnp.sin(ang)
    ch, sh = cs(HEAD_DIM // 2)
    ci, si = cs(IDX_DIM // 2)
    n = pos.shape[0]
    one = jnp.ones((n, LANES - IDX_DIM), F32)
    return jnp.stack([
        jnp.concatenate([ch, ch], axis=1), jnp.concatenate([-sh, sh], axis=1),
        jnp.concatenate([ci, ci, ci, ci], axis=1), jnp.concatenate([-si, si, -si, si], axis=1),
        jnp.concatenate([ci, ci, one], axis=1), jnp.concatenate([-si, si, 0.0 * one], axis=1)])


def _pool_kernel(up_ref, zp_ref, hist_ref, wmix_ref, scale_ref, o_ref, ext_ref, *, tm, pos0):
    t = pl.program_id(1)

    @pl.when(t == 0)
    def _():
        ext_ref[0:HIST_ROWS, :] = hist_ref[0]

    @pl.when(t > 0)
    def _():
        ext_ref[0:HIST_ROWS, :] = ext_ref[tm:tm + HIST_ROWS, :]

    u = up_ref[0]
    ext_ref[HIST_ROWS:HIST_ROWS + tm, :] = u
    pos = pos0 + t * tm + lax.broadcasted_iota(I32, (tm, 1), 0)
    for g, w in enumerate(POOL_WINDOWS):
        sl = slice(g * POOL_GROUP, (g + 1) * POOL_GROUP)
        acc = u[:, sl]
        for j in range(1, w):
            acc = acc + ext_ref[HIST_ROWS - j:HIST_ROWS - j + tm, sl]
        cnt = jnp.minimum(pos + 1, w).astype(F32)
        d = acc / cnt - u[:, sl]
        mixed = jnp.dot(d.astype(BF16), wmix_ref[g], preferred_element_type=F32)
        o_ref[0, :, sl] = (mixed * scale_ref[:, sl] * _silu(zp_ref[0, :, sl])).astype(BF16)


def _pool(h3, hist, wmix, scale, tm, pos0):
    b, t, _ = h3.shape
    width = scale.shape[1]
    return pl.pallas_call(
        functools.partial(_pool_kernel, tm=tm, pos0=pos0),
        out_shape=jax.ShapeDtypeStruct((b, t, width), BF16),
        grid=(b, t // tm),
        in_specs=[pl.BlockSpec((1, tm, W_SEG), lambda i, j: (i, j, SEG_UP)),
                  pl.BlockSpec((1, tm, W_SEG), lambda i, j: (i, j, SEG_ZP)),
                  pl.BlockSpec((1, HIST_ROWS, width), lambda i, j: (i, 0, 0)),
                  pl.BlockSpec(wmix.shape, lambda i, j: (0, 0, 0)),
                  pl.BlockSpec((1, width), lambda i, j: (0, 0))],
        out_specs=pl.BlockSpec((1, tm, width), lambda i, j: (i, j, 0)),
        scratch_shapes=[pltpu.VMEM((HIST_ROWS + tm, width), F32)],
        compiler_params=_cparams(("parallel", "arbitrary"), 32),
        name="pool_mixer",
    )(h3, h3, hist, wmix, scale)


def _float_key(x):
    bits = pltpu.bitcast(x, I32)
    return bits ^ ((bits >> 31) & jnp.int32(0x7FFFFFFF))


def _select_topk(count_ge, count_gt_eq, count_eq_below, shape, k_top, idx_bits, j_ref):
    def bit_body(i, t_u):
        c_u = t_u | (jnp.int32(1) << (31 - i))
        cnt = count_ge(c_u ^ jnp.int32(INT_MIN))
        return jnp.where(cnt >= k_top, c_u, t_u)

    thr = lax.fori_loop(0, 32, bit_body, jnp.zeros(shape, I32)) ^ jnp.int32(INT_MIN)
    n_gt, n_eq = count_gt_eq(thr)
    need = k_top - n_gt
    j_ref[...] = jnp.full(shape, (1 << idx_bits) - 1, I32)

    @pl.when(jnp.max(n_eq - need) > 0.0)
    def _():
        def idx_body(i, j):
            c = j | (jnp.int32(1) << (idx_bits - 1 - i))
            return jnp.where(count_eq_below(thr, c) < need, c, j)
        j_ref[...] = lax.fori_loop(0, idx_bits, idx_body, jnp.zeros(shape, I32))

    return thr


def _count(mask, axis):
    return jnp.sum(jnp.where(mask, 1.0, 0.0), axis=axis, keepdims=True)


def _prompt_attn_kernel(q_ref, k_ref, v_ref, iq_ref, ike_ref, iko_ref, sm_ref, za_ref, o_ref,
                        key_ref, bias_t_ref, bias_ref, j_ref, *, seq, k_top, chunk):
    s0 = pl.program_id(1) * Q_BLOCK
    sm_t = sm_ref[:, 0:LANES].T
    w_t = sm_t[IDX_DIM:IDX_DIM + N_IDX_HEADS, :] * ((N_IDX_HEADS * IDX_DIM) ** -0.5)
    qpos = s0 + lax.broadcasted_iota(I32, (1, Q_BLOCK), 1)

    def score_chunk(c, carry):
        r0 = pl.multiple_of(c * chunk, chunk)
        ike = ike_ref[pl.ds(r0, chunk), :]
        iko = iko_ref[pl.ds(r0, chunk), :]
        acc = jnp.zeros((chunk, Q_BLOCK), F32)
        for p in range(N_IDX_HEADS // 2):
            rhs = iq_ref[:, p * LANES:(p + 1) * LANES]
            lg_e = lax.dot_general(ike, rhs, NT_DIMS, preferred_element_type=F32)
            lg_o = lax.dot_general(iko, rhs, NT_DIMS, preferred_element_type=F32)
            acc = acc + jnp.maximum(lg_e, 0.0) * w_t[2 * p:2 * p + 1, :]
            acc = acc + jnp.maximum(lg_o, 0.0) * w_t[2 * p + 1:2 * p + 2, :]
        kidx = r0 + lax.broadcasted_iota(I32, (chunk, 1), 0)
        key_ref[pl.ds(r0, chunk), :] = jnp.where(kidx <= qpos, _float_key(acc), jnp.int32(INT_MIN))
        return carry

    lax.fori_loop(0, seq // chunk, score_chunk, 0)

    def key_index():
        return lax.broadcasted_iota(I32, (seq, 1), 0)

    def count_ge(c):
        return _count(key_ref[...] >= c, 0)

    def count_gt_eq(thr):
        keys = key_ref[...]
        return _count(keys > thr, 0), _count((keys == thr) & (key_index() <= qpos), 0)

    def count_eq_below(thr, c):
        kidx = key_index()
        return _count((key_ref[...] == thr) & (kidx <= qpos) & (kidx < c), 0)

    thr = _select_topk(count_ge, count_gt_eq, count_eq_below, (1, Q_BLOCK), float(k_top),
                       (seq - 1).bit_length(), j_ref)
    keys = key_ref[...]
    kidx = key_index()
    sel = ((keys > thr) | ((keys == thr) & (kidx <= j_ref[...]))) & (kidx <= qpos)
    bias_t_ref[...] = jnp.where(sel, 0.0, MASK_NEG)
    bias_ref[...] = bias_t_ref[...].T

    scale = HEAD_DIM ** -0.5
    for h in range(N_HEADS):
        sl = slice(h * HEAD_DIM, (h + 1) * HEAD_DIM)
        s = lax.dot_general(q_ref[:, sl], k_ref[:, sl], NT_DIMS, preferred_element_type=F32)
        s = s * scale + bias_ref[...]
        m = jnp.max(s, axis=-1, keepdims=True)
        p = jnp.exp(s - m)
        l = jnp.sum(p, axis=-1, keepdims=True)
        o = jnp.dot(p.astype(BF16), v_ref[:, sl], preferred_element_type=F32) / l
        o_ref[:, sl] = (o * _silu(za_ref[:, sl])).astype(BF16)


def _prompt_attention(qb, kb, vb, iqb, ike, iko, h, batch, seq):
    m = qb.shape[0]
    nq = seq // Q_BLOCK
    k_top = min(TOPK_MAX, seq // 4)
    qrow = lambda w: pl.BlockSpec((Q_BLOCK, w), lambda b, i: (b * nq + i, 0))
    kv = lambda w: pl.BlockSpec((seq, w), lambda b, i: (b, 0))
    return pl.pallas_call(
        functools.partial(_prompt_attn_kernel, seq=seq, k_top=k_top, chunk=256),
        out_shape=jax.ShapeDtypeStruct((m, W_SEG), BF16),
        grid=(batch, nq),
        in_specs=[qrow(W_SEG), kv(W_SEG), kv(W_SEG), qrow(W_SEG), kv(LANES), kv(LANES),
                  pl.BlockSpec((Q_BLOCK, SMALL_W), lambda b, i: (b * nq + i, SEG_SMALL)),
                  pl.BlockSpec((Q_BLOCK, W_SEG), lambda b, i: (b * nq + i, SEG_ZA))],
        out_specs=qrow(W_SEG),
        scratch_shapes=[pltpu.VMEM((seq, Q_BLOCK), I32), pltpu.VMEM((seq, Q_BLOCK), F32),
                        pltpu.VMEM((Q_BLOCK, seq), F32), pltpu.VMEM((1, Q_BLOCK), I32)],
        compiler_params=_cparams(("parallel", "arbitrary"), 48),
        name="prompt_sparse_attn",
    )(qb, kb, vb, iqb, ike, iko, h, h)


def _sample_scores_kernel(pt_ref, iqr_ref, wcol_ref, *rest, n_pages):
    del pt_ref
    page_refs, o_ref = rest[:n_pages], rest[n_pages]
    iqr = iqr_ref[0]
    w = wcol_ref[0] * ((N_IDX_HEADS * IDX_DIM) ** -0.5)
    for j in range(n_pages):
        ikp = page_refs[j][0].astype(BF16)
        lg = lax.dot_general(iqr, ikp, NT_DIMS, preferred_element_type=F32)
        t = jnp.maximum(lg, 0.0) * w
        sc = t.reshape(SAMPLE_ROWS, N_IDX_HEADS, PAGE_SIZE).sum(axis=1)
        o_ref[0, :, j * PAGE_SIZE:(j + 1) * PAGE_SIZE] = 0.0 + sc


def _sample_scores(page_table, iqr, wcol, pages, n_chunks, n_pages, paged):
    db = iqr.shape[0]
    if paged:
        page_spec = lambda j: pl.BlockSpec(
            (1, PAGE_SIZE, IDX_DIM), lambda b, c, pt, j=j: (pt[b, c * n_pages + j], 0, 0))
    else:
        page_spec = lambda j: pl.BlockSpec((1, PAGE_SIZE, IDX_DIM), lambda b, c, pt: (b, 0, 0))
    rows = SAMPLE_ROWS * N_IDX_HEADS
    return pl.pallas_call(
        functools.partial(_sample_scores_kernel, n_pages=n_pages),
        out_shape=jax.ShapeDtypeStruct((db, SAMPLE_ROWS, n_chunks * n_pages * PAGE_SIZE), F32),
        grid_spec=pltpu.PrefetchScalarGridSpec(
            num_scalar_prefetch=1,
            grid=(db, n_chunks),
            in_specs=[pl.BlockSpec((1, rows, IDX_DIM), lambda b, c, pt: (b, 0, 0)),
                      pl.BlockSpec((1, rows, 1), lambda b, c, pt: (b, 0, 0))]
                     + [page_spec(j) for j in range(n_pages)],
            out_specs=pl.BlockSpec((1, SAMPLE_ROWS, n_pages * PAGE_SIZE), lambda b, c, pt: (b, 0, c))),
        compiler_params=_cparams(("parallel", "arbitrary"), 32),
        name="sample_idx_scores",
    )(page_table, iqr, wcol, *([pages] * n_pages))


def _sample_select_kernel(sp_ref, sn_ref, bp_ref, bn_ref, keyp_ref, keyn_ref, j_ref, *, k_top, past):
    rows = sp_ref.shape[0]
    n = lax.broadcasted_iota(I32, (rows, 1), 0) % SAMPLE_ROWS
    colp = lax.broadcasted_iota(I32, (1, past), 1)
    coln = lax.broadcasted_iota(I32, (1, PAGE_SIZE), 1)
    validn = coln <= n
    keyp_ref[...] = _float_key(sp_ref[...])
    keyn_ref[...] = jnp.where(validn, _float_key(sn_ref[...]), jnp.int32(INT_MIN))

    def count_ge(c):
        return _count(keyp_ref[...] >= c, 1) + _count(keyn_ref[...] >= c, 1)

    def count_gt_eq(thr):
        kp, kn = keyp_ref[...], keyn_ref[...]
        return (_count(kp > thr, 1) + _count(kn > thr, 1),
                _count(kp == thr, 1) + _count((kn == thr) & validn, 1))

    def count_eq_below(thr, c):
        return (_count((keyp_ref[...] == thr) & (colp < c), 1)
                + _count((keyn_ref[...] == thr) & validn & (coln + past < c), 1))

    thr = _select_topk(count_ge, count_gt_eq, count_eq_below, (rows, 1), float(k_top),
                       (past + PAGE_SIZE - 1).bit_length(), j_ref)
    j = j_ref[...]
    kp, kn = keyp_ref[...], keyn_ref[...]
    bp_ref[...] = jnp.where((kp > thr) | ((kp == thr) & (colp <= j)), 0.0, MASK_NEG)
    sel_n = ((kn > thr) | ((kn == thr) & (coln + past <= j))) & validn
    bn_ref[...] = jnp.where(sel_n, 0.0, MASK_NEG)


def _sample_select(sc_past, sc_new, k_top):
    rows, past = sc_past.shape
    tr = 2 * SAMPLE_ROWS
    blk = lambda w: pl.BlockSpec((tr, w), lambda i: (i, 0))
    return pl.pallas_call(
        functools.partial(_sample_select_kernel, k_top=k_top, past=past),
        out_shape=(jax.ShapeDtypeStruct(sc_past.shape, F32), jax.ShapeDtypeStruct(sc_new.shape, F32)),
        grid=(rows // tr,),
        in_specs=[blk(past), blk(PAGE_SIZE)],
        out_specs=(blk(past), blk(PAGE_SIZE)),
        scratch_shapes=[pltpu.VMEM((tr, past), I32), pltpu.VMEM((tr, PAGE_SIZE), I32),
                        pltpu.VMEM((tr, 1), I32)],
        compiler_params=_cparams(("parallel",), 48),
        name="sample_topk_select",
    )(sc_past, sc_new)


def _sample_attn_kernel(pt_ref, q_ref, bias_ref, knew_ref, vnew_ref, bnew_ref, za_ref, *rest,
                        n_pages, n_chunks):
    del pt_ref
    k_refs, v_refs = rest[:n_pages], rest[n_pages:2 * n_pages]
    o_ref, m_ref, l_ref, acc_ref = rest[2 * n_pages:]
    c = pl.program_id(1)
    scale = HEAD_DIM ** -0.5

    @pl.when(c == 0)
    def _():
        m_ref[...] = jnp.full(m_ref.shape, M_INIT, F32)
        l_ref[...] = jnp.zeros(l_ref.shape, F32)
        acc_ref[...] = jnp.zeros(acc_ref.shape, F32)

    def process(k_pages, v_pages, bias):
        for h in range(N_HEADS):
            sl = slice(h * HEAD_DIM, (h + 1) * HEAD_DIM)
            qh = q_ref[:, sl]
            s = jnp.concatenate(
                [lax.dot_general(qh, kp[0, :, sl].astype(BF16), NT_DIMS, preferred_element_type=F32)
                 for kp in k_pages], axis=1)
            s = s * scale + bias
            m_old = m_ref[h]
            m_new = jnp.maximum(m_old, jnp.max(s, axis=-1, keepdims=True))
            alpha = jnp.exp(m_old - m_new)
            p = jnp.exp(s - m_new[:, 0:1])
            l_ref[h] = alpha * l_ref[h] + jnp.sum(p, axis=-1, keepdims=True)
            pv = jnp.zeros((SAMPLE_ROWS, HEAD_DIM), F32)
            for j, vp in enumerate(v_pages):
                pv = pv + jnp.dot(p[:, j * PAGE_SIZE:(j + 1) * PAGE_SIZE].astype(BF16),
                                  vp[0, :, sl].astype(BF16), preferred_element_type=F32)
            acc_ref[h] = alpha * acc_ref[h] + pv
            m_ref[h] = m_new

    @pl.when(c < n_chunks)
    def _():
        process(k_refs, v_refs, bias_ref[0])

    @pl.when(c == n_chunks)
    def _():
        process([knew_ref], [vnew_ref], bnew_ref[0])
        for h in range(N_HEADS):
            sl = slice(h * HEAD_DIM, (h + 1) * HEAD_DIM)
            o = acc_ref[h] / l_ref[h]
            o_ref[:, sl] = (o * _silu(za_ref[:, sl])).astype(BF16)


def _sample_attention(page_table, qb, bias_past, k_new, v_new, bias_new, h, cache_k, cache_v):
    db, n_table_pages = page_table.shape
    n_pages = PAGES_PER_STEP
    n_chunks = n_table_pages // n_pages
    last = n_table_pages - 1
    page_spec = lambda j: pl.BlockSpec(
        (1, PAGE_SIZE, W_SEG),
        lambda b, c, pt, j=j: (pt[b, jnp.minimum(c * n_pages + j, last)], 0, 0))
    per_seq = lambda shape: pl.BlockSpec(shape, lambda b, c, pt: (b, 0, 0))
    return pl.pallas_call(
        functools.partial(_sample_attn_kernel, n_pages=n_pages, n_chunks=n_chunks),
        out_shape=jax.ShapeDtypeStruct((db * SAMPLE_ROWS, W_SEG), BF16),
        grid_spec=pltpu.PrefetchScalarGridSpec(
            num_scalar_prefetch=1,
            grid=(db, n_chunks + 1),
            in_specs=[pl.BlockSpec((SAMPLE_ROWS, W_SEG), lambda b, c, pt: (b, 0)),
                      pl.BlockSpec((1, SAMPLE_ROWS, n_pages * PAGE_SIZE),
                                   lambda b, c, pt: (b, 0, jnp.minimum(c, n_chunks - 1))),
                      per_seq((1, PAGE_SIZE, W_SEG)), per_seq((1, PAGE_SIZE, W_SEG)),
                      per_seq((1, SAMPLE_ROWS, PAGE_SIZE)),
                      pl.BlockSpec((SAMPLE_ROWS, W_SEG), lambda b, c, pt: (b, SEG_ZA))]
                     + [page_spec(j) for j in range(n_pages)] * 2,
            out_specs=pl.BlockSpec((SAMPLE_ROWS, W_SEG), lambda b, c, pt: (b, 0)),
            scratch_shapes=[pltpu.VMEM((N_HEADS, SAMPLE_ROWS, LANES), F32),
                            pltpu.VMEM((N_HEADS, SAMPLE_ROWS, LANES), F32),
                            pltpu.VMEM((N_HEADS, SAMPLE_ROWS, HEAD_DIM), F32)]),
        compiler_params=_cparams(("parallel", "arbitrary"), 48),
        name="sample_sparse_attn",
    )(page_table, qb, bias_past, k_new, v_new, bias_new, h,
      *([cache_k] * n_pages), *([cache_v] * n_pages))


def _merge_kernel(x_ref, pg_ref, ag_ref, gp_ref, ga_ref, wpo_ref, wao_ref, wo_ref, g_ref, b_ref,
                  o_ref, *, alpha):
    p = jnp.dot(pg_ref[...], wpo_ref[...], preferred_element_type=F32)
    a = jnp.dot(ag_ref[...], wao_ref[...], preferred_element_type=F32)
    m = jax.nn.sigmoid(gp_ref[...]) * p + jax.nn.sigmoid(ga_ref[...]) * a
    y = jnp.dot(m.astype(BF16), wo_ref[...], preferred_element_type=F32)
    o_ref[...] = _layer_norm_rows(alpha * x_ref[...] + y, g_ref[...], b_ref[...])


def _merge(x, pg, ag, h, wpo, wao, wo, g, b, tm, alpha):
    m, d = x.shape
    const = lambda a: pl.BlockSpec(a.shape, lambda i: (0, 0), pipeline_mode=pl.Buffered(1))
    return pl.pallas_call(
        functools.partial(_merge_kernel, alpha=alpha),
        out_shape=jax.ShapeDtypeStruct((m, d), F32),
        grid=(m // tm,),
        in_specs=[pl.BlockSpec((tm, d), lambda i: (i, 0)),
                  pl.BlockSpec((tm, W_SEG), lambda i: (i, 0)),
                  pl.BlockSpec((tm, W_SEG), lambda i: (i, 0)),
                  pl.BlockSpec((tm, d), lambda i: (i, SEG_GP)),
                  pl.BlockSpec((tm, d), lambda i: (i, SEG_GA)),
                  const(wpo), const(wao), const(wo),
                  pl.BlockSpec((1, d), lambda i: (0, 0)),
                  pl.BlockSpec((1, d), lambda i: (0, 0))],
        out_specs=pl.BlockSpec((tm, d), lambda i: (i, 0)),
        compiler_params=_cparams(("parallel",), 56),
        name="merge_out_ln",
    )(x, pg, ag, h, h, wpo, wao, wo, g.reshape(1, d), b.reshape(1, d))


def _pack_w_in(w):
    d = w.shape[0]
    n_main = 7 * W_SEG
    n_small = IDX_DIM + N_IDX_HEADS
    pad = jnp.zeros((d, SMALL_W - n_small), w.dtype)
    gates = w[:, n_main + n_small:]
    return jnp.concatenate([gates, w[:, :n_main], w[:, n_main:n_main + n_small], pad],
                           axis=1).astype(BF16)


def kernel(x_prompt, x_sample, cache_k, cache_v, cache_idx_k, state_pool, page_table, ln_emb_g, ln_emb_b, w_in, w_pool_mix, pool_scale, w_pool_out, w_attn_out, w_o, ln_g, ln_b):
    batch, seq, d_model = x_prompt.shape
    dec_batch, dec_seq, _ = x_sample.shape
    depth = w_in.shape[0]
    n_table_pages = page_table.shape[1]
    past = n_table_pages * PAGE_SIZE
    n_pool_pages = cache_k.shape[1]
    alpha = (2 * depth) ** 0.25
    assert w_in.shape[2] == H_WIDTH - (SMALL_W - IDX_DIM - N_IDX_HEADS)
    assert seq % 512 == 0 and dec_seq <= SAMPLE_ROWS and n_table_pages % PAGES_PER_STEP == 0
    mp = batch * seq
    ms = dec_batch * SAMPLE_ROWS

    xp = _layer_norm(x_prompt.reshape(mp, d_model), ln_emb_g, ln_emb_b, 256)
    xs_in = jnp.pad(x_sample, ((0, 0), (0, SAMPLE_ROWS - dec_seq), (0, 0))).reshape(ms, d_model)
    xs = _layer_norm(xs_in, ln_emb_g, ln_emb_b, ms)

    tab_p = _rope_tables(jnp.arange(seq, dtype=I32))
    tab_s = _rope_tables(jnp.tile(past + jnp.arange(SAMPLE_ROWS, dtype=I32), dec_batch))
    k_top_s = min(TOPK_MAX, (past + dec_seq) // 4)

    kp, vp, ikp, plp, kss, vss, iks, pls = [], [], [], [], [], [], [], []
    for l in range(depth):
        w_all = _pack_w_in(w_in[l])
        wmix = w_pool_mix[l].astype(BF16)
        scale = pool_scale[l].reshape(1, -1)
        wpo, wao, wo = w_pool_out[l].astype(BF16), w_attn_out[l].astype(BF16), w_o[l].astype(BF16)

        h = _project(xp, w_all, 512, 1280)
        k_f, v_f, qb, kb, vb, iqb, ik_f, ike, iko = _rope(h, tab_p, 256, seq // 256)
        h3 = h.reshape(batch, seq, H_WIDTH)
        pg = _pool(h3, jnp.zeros((batch, HIST_ROWS, W_SEG), F32), wmix, scale, 256, 0)
        ag = _prompt_attention(qb, kb, vb, iqb, ike, iko, h, batch, seq)
        kp.append(k_f.reshape(batch, seq, N_HEADS, HEAD_DIM))
        vp.append(v_f.reshape(batch, seq, N_HEADS, HEAD_DIM))
        ikp.append(ik_f[:, :IDX_DIM].reshape(batch, seq, IDX_DIM))
        plp.append(h3[:, seq - POOL_BUF:, SEG_UP * W_SEG:(SEG_UP + 1) * W_SEG])
        xp = _merge(xp, pg.reshape(mp, W_SEG), ag, h, wpo, wao, wo, ln_g[l], ln_b[l], 256, alpha)

        hs = _project(xs, w_all, ms, 1280)
        k_f, v_f, qb, kb, vb, iqb, ik_f, ike, iko = _rope(hs, tab_s, ms, 1)
        hs3 = hs.reshape(dec_batch, SAMPLE_ROWS, H_WIDTH)
        hist = jnp.pad(state_pool[l], ((0, 0), (HIST_ROWS - POOL_BUF, 0), (0, 0)))
        pg = _pool(hs3, hist, wmix, scale, SAMPLE_ROWS, past)
        iqr = iqb.reshape(dec_batch, SAMPLE_ROWS * N_IDX_HEADS, IDX_DIM)
        wcol = ik_f[:, IDX_DIM:IDX_DIM + N_IDX_HEADS].reshape(dec_batch, SAMPLE_ROWS * N_IDX_HEADS, 1)
        ik_new = jnp.pad(ik_f[:, :IDX_DIM].reshape(dec_batch, SAMPLE_ROWS, IDX_DIM),
                         ((0, 0), (0, PAGE_SIZE - SAMPLE_ROWS), (0, 0)))
        sc_past = _sample_scores(page_table, iqr, wcol, cache_idx_k[l],
                                 n_table_pages // PAGES_PER_STEP, PAGES_PER_STEP, True)
        sc_new = _sample_scores(page_table, iqr, wcol, ik_new, 1, 1, False)
        bias_past, bias_new = _sample_select(sc_past.reshape(ms, past), sc_new.reshape(ms, PAGE_SIZE),
                                             k_top_s)
        pad_page = lambda a: jnp.pad(a.reshape(dec_batch, SAMPLE_ROWS, W_SEG),
                                     ((0, 0), (0, PAGE_SIZE - SAMPLE_ROWS), (0, 0)))
        ag = _sample_attention(page_table, qb, bias_past.reshape(dec_batch, SAMPLE_ROWS, past),
                               pad_page(k_f), pad_page(v_f),
                               bias_new.reshape(dec_batch, SAMPLE_ROWS, PAGE_SIZE), hs,
                               cache_k[l].reshape(n_pool_pages, PAGE_SIZE, W_SEG),
                               cache_v[l].reshape(n_pool_pages, PAGE_SIZE, W_SEG))
        real = lambda a, *tail: a.reshape((dec_batch, SAMPLE_ROWS) + tail)[:, :dec_seq]
        kss.append(real(k_f, N_HEADS, HEAD_DIM))
        vss.append(real(v_f, N_HEADS, HEAD_DIM))
        iks.append(real(ik_f, LANES)[:, :, :IDX_DIM])
        us = hs3[:, :dec_seq, SEG_UP * W_SEG:(SEG_UP + 1) * W_SEG]
        pls.append(jnp.concatenate([state_pool[l], us], axis=1)[:, -POOL_BUF:])
        xs = _merge(xs, pg.reshape(ms, W_SEG), ag, hs, wpo, wao, wo, ln_g[l], ln_b[l], ms, alpha)

    y_sample = xs.reshape(dec_batch, SAMPLE_ROWS, d_model)[:, :dec_seq]
    return (xp.reshape(batch, seq, d_model), y_sample, jnp.stack(kp), jnp.stack(vp), jnp.stack(ikp),
            jnp.stack(plp), jnp.stack(kss), jnp.stack(vss), jnp.stack(iks), jnp.stack(pls))
```

```python
import functools
import math

import jax
import jax.numpy as jnp
from jax import lax
from jax.experimental import pallas as pl
from jax.experimental.pallas import tpu as pltpu

F32 = jnp.float32
BF16 = jnp.bfloat16
I32 = jnp.int32

N_HEADS = 8
HEAD_DIM = 128
N_IDX_HEADS = 16
IDX_DIM = 64
IDX_SCALE = (N_IDX_HEADS * IDX_DIM) ** -0.5
POOL_WINDOWS = (2, 4, 8, 16)
POOL_GROUP = 256
POOL_BUF = 15
TOPK_MAX = 256
ROPE_THETA = 10000.0
LN_EPS = 1e-5
PAGE_SIZE = 128

LANES = 128
HIST_ROWS = 16
SAMPLE_ROWS = 16
SMALL_W = 256
Q_TILE = 256
K_TILE = 256
SCORE_ROWS = 128
ACC_ROWS = 64
SMALL_ACC_ROWS = 8
SCORE_PAGES_PER_STEP = 16
ATTN_PAGES_PER_STEP = 8

W_SEG = 1024
SEG_GP, SEG_GA = 0, 1
SEG_UP, SEG_ZP, SEG_Q, SEG_K, SEG_V, SEG_ZA, SEG_IQ = 4, 5, 6, 7, 8, 9, 10
SEG_SMALL = 44
H_WIDTH = 11 * W_SEG + SMALL_W

INT_MIN = -2 ** 31
NT_DIMS = (((1,), (1,)), ((), ()))
MASK_NEG = float("-inf")
M_INIT = -1e30
EXP2_SCALE = HEAD_DIM ** -0.5 * math.log2(math.e)


def _cparams(sem, vmem_mb):
    return pltpu.CompilerParams(dimension_semantics=sem, vmem_limit_bytes=vmem_mb * 1024 * 1024)


def _silu(z):
    return z * jax.nn.sigmoid(z)


def _layer_norm_rows(x, g, b):
    mu = jnp.mean(x, axis=-1, keepdims=True)
    xc = x - mu
    var = jnp.mean(xc * xc, axis=-1, keepdims=True)
    return xc * lax.rsqrt(var + LN_EPS) * g + b


def _ln_kernel(x_ref, g_ref, b_ref, o_ref):
    o_ref[...] = _layer_norm_rows(x_ref[...], g_ref[...], b_ref[...])


def _layer_norm(x, g, b, tm):
    m, d = x.shape
    return pl.pallas_call(
        _ln_kernel,
        out_shape=jax.ShapeDtypeStruct((m, d), F32),
        grid=(m // tm,),
        in_specs=[pl.BlockSpec((tm, d), lambda i: (i, 0)),
                  pl.BlockSpec((1, d), lambda i: (0, 0)),
                  pl.BlockSpec((1, d), lambda i: (0, 0))],
        out_specs=pl.BlockSpec((tm, d), lambda i: (i, 0)),
        compiler_params=_cparams(("parallel",), 32),
        name="ln_embed",
    )(x, g.reshape(1, d), b.reshape(1, d))


def _proj_kernel(x_ref, w_ref, o_ref, xb_ref):
    @pl.when(pl.program_id(1) == 0)
    def _():
        xb_ref[...] = x_ref[...].astype(BF16)

    o_ref[...] = jnp.dot(xb_ref[...], w_ref[...], preferred_element_type=F32)


def _project(x, w, tm, tn):
    m, d = x.shape
    n = w.shape[1]
    return pl.pallas_call(
        _proj_kernel,
        out_shape=jax.ShapeDtypeStruct((m, n), F32),
        grid=(m // tm, n // tn),
        in_specs=[pl.BlockSpec((tm, d), lambda i, j: (i, 0)),
                  pl.BlockSpec((d, tn), lambda i, j: (0, j))],
        out_specs=pl.BlockSpec((tm, tn), lambda i, j: (i, j)),
        scratch_shapes=[pltpu.VMEM((tm, d), BF16)],
        compiler_params=_cparams(("parallel", "arbitrary"), 48),
        name="in_proj",
    )(x, w)


def _rope_kernel(q_ref, k_ref, v_ref, iq_ref, sm_ref, tab_ref,
                 ko_ref, vo_ref, qb_ref, kb_ref, vt_ref, iqb_ref, iks_ref, ike_ref, iko_ref):
    cos_h, sin_h = tab_ref[0], tab_ref[1]
    cos_i, sin_i = tab_ref[2], tab_ref[3]
    cos_s, sin_s = tab_ref[4], tab_ref[5]
    lane = lax.broadcasted_iota(I32, cos_h.shape, 1)
    low_half = (lane % IDX_DIM) < (IDX_DIM // 2)

    def rope_idx(x, cos, sin):
        partner = jnp.where(low_half, pltpu.roll(x, LANES - IDX_DIM // 2, 1),
                            pltpu.roll(x, IDX_DIM // 2, 1))
        return x * cos + partner * sin

    for h in range(N_HEADS):
        sl = slice(h * HEAD_DIM, (h + 1) * HEAD_DIM)
        x = q_ref[:, sl]
        qb_ref[:, sl] = (x * cos_h + pltpu.roll(x, HEAD_DIM // 2, 1) * sin_h).astype(BF16)
        x = k_ref[:, sl]
        kr = x * cos_h + pltpu.roll(x, HEAD_DIM // 2, 1) * sin_h
        ko_ref[:, sl] = kr
        kb_ref[:, sl] = kr.astype(BF16)
        iqb_ref[:, sl] = rope_idx(iq_ref[:, sl], cos_i, sin_i).astype(BF16)
        v = v_ref[:, sl]
        vo_ref[:, sl] = v
        vt_ref[0, sl, :] = v.T.astype(BF16)
    xs = rope_idx(sm_ref[:, 0:LANES], cos_s, sin_s)
    iks_ref[...] = xs
    ik_only = jnp.where(lane < IDX_DIM, xs, 0.0)
    ike_ref[...] = ik_only.astype(BF16)
    iko_ref[...] = pltpu.roll(ik_only, IDX_DIM, 1).astype(BF16)


def _rope(h, tab, tm, tab_blocks):
    m = h.shape[0]
    seg = lambda s: pl.BlockSpec((tm, W_SEG), lambda i, s=s: (i, s))
    row = lambda w: pl.BlockSpec((tm, w), lambda i: (i, 0))
    f32o = lambda w: jax.ShapeDtypeStruct((m, w), F32)
    b16o = lambda w: jax.ShapeDtypeStruct((m, w), BF16)
    return pl.pallas_call(
        _rope_kernel,
        out_shape=(f32o(W_SEG), f32o(W_SEG), b16o(W_SEG), b16o(W_SEG),
                   jax.ShapeDtypeStruct((m // tm, W_SEG, tm), BF16), b16o(W_SEG),
                   f32o(LANES), b16o(LANES), b16o(LANES)),
        grid=(m // tm,),
        in_specs=[seg(SEG_Q), seg(SEG_K), seg(SEG_V), seg(SEG_IQ),
                  pl.BlockSpec((tm, SMALL_W), lambda i: (i, SEG_SMALL)),
                  pl.BlockSpec((6, tm, LANES), lambda i: (0, i % tab_blocks, 0))],
        out_specs=(row(W_SEG), row(W_SEG), row(W_SEG), row(W_SEG),
                   pl.BlockSpec((1, W_SEG, tm), lambda i: (i, 0, 0)), row(W_SEG),
                   row(LANES), row(LANES), row(LANES)),
        compiler_params=_cparams(("parallel",), 48),
        name="rope_split",
    )(h, h, h, h, h, tab)


def _rope_tables(pos):
    def cs(half):
        inv = ROPE_THETA ** (-jnp.arange(half, dtype=F32) / half)
        ang = pos.astype(F32)[:, None] * inv[None, :]
        return jnp.cos(ang), jnp.sin(ang)
    ch, sh = cs(HEAD_DIM // 2)
    ci, si = cs(IDX_DIM // 2)
    n = pos.shape[0]
    one = jnp.ones((n, LANES - IDX_DIM), F32)
    return jnp.stack([
        jnp.concatenate([ch, ch], axis=1), jnp.concatenate([-sh, sh], axis=1),
        jnp.concatenate([ci, ci, ci, ci], axis=1), jnp.concatenate([-si, si, -si, si], axis=1),
        jnp.concatenate([ci, ci, one], axis=1), jnp.concatenate([-si, si, 0.0 * one], axis=1)])


def _pool_kernel(up_ref, zp_ref, hist_ref, wmix_ref, scale_ref, o_ref, ext_ref, *, tm, pos0):
    t = pl.program_id(1)

    @pl.when(t == 0)
    def _():
        ext_ref[0:HIST_ROWS, :] = hist_ref[0]

    @pl.when(t > 0)
    def _():
        ext_ref[0:HIST_ROWS, :] = ext_ref[tm:tm + HIST_ROWS, :]

    u = up_ref[0]
    ext_ref[HIST_ROWS:HIST_ROWS + tm, :] = u
    pos = pos0 + t * tm + lax.broadcasted_iota(I32, (tm, 1), 0)
    for g, w in enumerate(POOL_WINDOWS):
        sl = slice(g * POOL_GROUP, (g + 1) * POOL_GROUP)
        acc = u[:, sl]
        for j in range(1, w):
            acc = acc + ext_ref[HIST_ROWS - j:HIST_ROWS - j + tm, sl]
        cnt = jnp.minimum(pos + 1, w).astype(F32)
        d = acc / cnt - u[:, sl]
        mixed = jnp.dot(d.astype(BF16), wmix_ref[g], preferred_element_type=F32)
        o_ref[0, :, sl] = (mixed * scale_ref[:, sl] * _silu(zp_ref[0, :, sl])).astype(BF16)


def _pool(h3, hist, wmix, scale, tm, pos0):
    b, t, _ = h3.shape
    width = scale.shape[1]
    return pl.pallas_call(
        functools.partial(_pool_kernel, tm=tm, pos0=pos0),
        out_shape=jax.ShapeDtypeStruct((b, t, width), BF16),
        grid=(b, t // tm),
        in_specs=[pl.BlockSpec((1, tm, W_SEG), lambda i, j: (i, j, SEG_UP)),
                  pl.BlockSpec((1, tm, W_SEG), lambda i, j: (i, j, SEG_ZP)),
                  pl.BlockSpec((1, HIST_ROWS, width), lambda i, j: (i, 0, 0)),
                  pl.BlockSpec(wmix.shape, lambda i, j: (0, 0, 0)),
                  pl.BlockSpec((1, width), lambda i, j: (0, 0))],
        out_specs=pl.BlockSpec((1, tm, width), lambda i, j: (i, j, 0)),
        scratch_shapes=[pltpu.VMEM((HIST_ROWS + tm, width), F32)],
        compiler_params=_cparams(("parallel", "arbitrary"), 32),
        name="pool_mixer",
    )(h3, h3, hist, wmix, scale)


def _float_key(x):
    bits = pltpu.bitcast(x, I32)
    return bits ^ ((bits >> 31) & jnp.int32(0x7FFFFFFF))


def _select_topk(count_ge, count_gt_eq, count_eq_below, shape, k_top, idx_bits, j_ref):
    def bit_body(i, t_u):
        c_u = t_u | (jnp.int32(1) << (31 - i))
        cnt = count_ge(c_u ^ jnp.int32(INT_MIN))
        return jnp.where(cnt >= k_top, c_u, t_u)

    thr = lax.fori_loop(0, 32, bit_body, jnp.zeros(shape, I32)) ^ jnp.int32(INT_MIN)
    n_gt, n_eq = count_gt_eq(thr)
    need = k_top - n_gt
    j_ref[...] = jnp.full(shape, (1 << idx_bits) - 1, I32)

    @pl.when(jnp.max(n_eq - need) > 0.0)
    def _():
        def idx_body(i, j):
            c = j | (jnp.int32(1) << (idx_bits - 1 - i))
            return jnp.where(count_eq_below(thr, c) < need, c, j)
        j_ref[...] = lax.fori_loop(0, idx_bits, idx_body, jnp.zeros(shape, I32))

    return thr


def _count_lanes(mask):
    x = jnp.where(mask, 1.0, 0.0)
    w = x.shape[1]
    while w > 8 * LANES and w % (2 * LANES) == 0:
        w //= 2
        x = x[:, :w] + x[:, w:]
    return jnp.sum(x, axis=1, keepdims=True)


def _prompt_attn_kernel(q_ref, k_ref, vt_ref, iq_ref, ike_ref, iko_ref, sm_ref, za_ref, o_ref,
                        key_ref, bias_ref, s_ref, o_acc_ref, j_ref, *, k_top, idx_bits):
    i = pl.program_id(1)
    n_chunks = i + 1
    qpos = i * Q_TILE + lax.broadcasted_iota(I32, (1, Q_TILE), 1)
    sm_t = sm_ref[:, 0:LANES].T
    w_t = sm_t[IDX_DIM:IDX_DIM + N_IDX_HEADS, :] * IDX_SCALE

    def score_chunk(c, carry):
        r0 = pl.multiple_of(c * SCORE_ROWS, SCORE_ROWS)
        ike = ike_ref[pl.ds(r0, SCORE_ROWS), :]
        iko = iko_ref[pl.ds(r0, SCORE_ROWS), :]
        acc = jnp.zeros((SCORE_ROWS, Q_TILE), F32)
        for p in range(N_IDX_HEADS // 2):
            rhs = iq_ref[:, p * LANES:(p + 1) * LANES]
            lg_e = lax.dot_general(ike, rhs, NT_DIMS, preferred_element_type=F32)
            lg_o = lax.dot_general(iko, rhs, NT_DIMS, preferred_element_type=F32)
            acc = acc + jnp.maximum(lg_e, 0.0) * w_t[2 * p:2 * p + 1, :]
            acc = acc + jnp.maximum(lg_o, 0.0) * w_t[2 * p + 1:2 * p + 2, :]
        kidx = r0 + lax.broadcasted_iota(I32, (SCORE_ROWS, 1), 0)
        key_ref[pl.ds(r0, SCORE_ROWS), :] = jnp.where(kidx <= qpos, _float_key(acc),
                                                      jnp.int32(INT_MIN))
        return carry

    lax.fori_loop(0, n_chunks * (K_TILE // SCORE_ROWS), score_chunk, 0)

    def fold(x, op):
        return op(x.reshape(K_TILE // ACC_ROWS, ACC_ROWS, Q_TILE), axis=0)

    def count_where(pred):
        def body(c, acc):
            r0 = pl.multiple_of(c * K_TILE, K_TILE)
            kidx = r0 + lax.broadcasted_iota(I32, (K_TILE, 1), 0)
            x = jnp.where(pred(key_ref[pl.ds(r0, K_TILE), :], kidx), 1.0, 0.0)
            return acc + fold(x, jnp.sum)
        acc = lax.fori_loop(0, n_chunks, body, jnp.zeros((ACC_ROWS, Q_TILE), F32))
        return jnp.sum(acc, axis=0, keepdims=True)

    def count_ge(c):
        return count_where(lambda keys, kidx: keys >= c)

    def count_gt_eq(thr):
        return (count_where(lambda keys, kidx: keys > thr),
                count_where(lambda keys, kidx: (keys == thr) & (kidx <= qpos)))

    def count_eq_below(thr, c):
        return count_where(lambda keys, kidx: (keys == thr) & (kidx <= qpos) & (kidx < c))

    thr = _select_topk(count_ge, count_gt_eq, count_eq_below, (1, Q_TILE), float(k_top),
                       idx_bits, j_ref)
    j_sel = j_ref[...]

    def bias_chunk(c, carry):
        r0 = pl.multiple_of(c * K_TILE, K_TILE)
        kidx = r0 + lax.broadcasted_iota(I32, (K_TILE, 1), 0)
        keys = key_ref[pl.ds(r0, K_TILE), :]
        sel = ((keys > thr) | ((keys == thr) & (kidx <= j_sel))) & (kidx <= qpos)
        bias_ref[pl.ds(r0, K_TILE), :] = jnp.where(sel, 0.0, MASK_NEG)
        return carry

    lax.fori_loop(0, n_chunks, bias_chunk, 0)

    heads = [slice(h * HEAD_DIM, (h + 1) * HEAD_DIM) for h in range(N_HEADS)]

    def fold_small(x, op):
        return op(x.reshape(K_TILE // SMALL_ACC_ROWS, SMALL_ACC_ROWS, Q_TILE), axis=0)

    def raw_scores(c, m_accs):
        r0 = pl.multiple_of(c * K_TILE, K_TILE)
        out = []
        for h, sl in enumerate(heads):
            s = lax.dot_general(k_ref[pl.ds(r0, K_TILE), sl], q_ref[:, sl], NT_DIMS,
                                preferred_element_type=F32)
            s = s + bias_ref[pl.ds(r0, K_TILE), :]
            s_ref[h, pl.ds(r0, K_TILE), :] = s
            out.append(jnp.maximum(m_accs[h], fold_small(s, jnp.max)))
        return tuple(out)

    m_accs = lax.fori_loop(
        0, n_chunks, raw_scores,
        tuple(jnp.full((SMALL_ACC_ROWS, Q_TILE), MASK_NEG, F32) for _ in heads))
    m_heads = [jnp.max(a, axis=0, keepdims=True) for a in m_accs]
    o_acc_ref[...] = jnp.zeros(o_acc_ref.shape, F32)

    def weighted_values(c, l_accs):
        r0 = pl.multiple_of(c * K_TILE, K_TILE)
        out = []
        for h, sl in enumerate(heads):
            p = jnp.exp2((s_ref[h, pl.ds(r0, K_TILE), :] - m_heads[h]) * EXP2_SCALE)
            o_acc_ref[h] += jnp.dot(vt_ref[c, sl, :], p.astype(BF16), preferred_element_type=F32)
            out.append(l_accs[h] + fold_small(p, jnp.sum))
        return tuple(out)

    l_accs = lax.fori_loop(
        0, n_chunks, weighted_values,
        tuple(jnp.zeros((SMALL_ACC_ROWS, Q_TILE), F32) for _ in heads))
    for h, sl in enumerate(heads):
        o_t = o_acc_ref[h] / jnp.sum(l_accs[h], axis=0, keepdims=True)
        o_ref[:, sl] = (o_t.T * _silu(za_ref[:, sl])).astype(BF16)


def _prompt_attention(qb, kb, vt, iqb, ike, iko, h, batch, seq):
    m = qb.shape[0]
    nq = seq // Q_TILE
    k_top = min(TOPK_MAX, seq // 4)
    qrow = lambda w: pl.BlockSpec((Q_TILE, w), lambda b, i: (b * nq + i, 0))
    kv = lambda w: pl.BlockSpec((seq, w), lambda b, i: (b, 0), pipeline_mode=pl.Buffered(1))
    return pl.pallas_call(
        functools.partial(_prompt_attn_kernel, k_top=k_top, idx_bits=(seq - 1).bit_length()),
        out_shape=jax.ShapeDtypeStruct((m, W_SEG), BF16),
        grid=(batch, nq),
        in_specs=[qrow(W_SEG), kv(W_SEG),
                  pl.BlockSpec((seq // K_TILE, W_SEG, K_TILE), lambda b, i: (b, 0, 0),
                               pipeline_mode=pl.Buffered(1)),
                  qrow(W_SEG), kv(LANES), kv(LANES),
                  pl.BlockSpec((Q_TILE, SMALL_W), lambda b, i: (b * nq + i, SEG_SMALL)),
                  pl.BlockSpec((Q_TILE, W_SEG), lambda b, i: (b * nq + i, SEG_ZA))],
        out_specs=qrow(W_SEG),
        scratch_shapes=[pltpu.VMEM((seq, Q_TILE), I32), pltpu.VMEM((seq, Q_TILE), F32),
                        pltpu.VMEM((N_HEADS, seq, Q_TILE), F32),
                        pltpu.VMEM((N_HEADS, HEAD_DIM, Q_TILE), F32),
                        pltpu.VMEM((1, Q_TILE), I32)],
        compiler_params=_cparams(("parallel", "arbitrary"), 56),
        name="prompt_sparse_attn",
    )(qb, kb, vt, iqb, ike, iko, h, h)


def _sample_scores_kernel(pt_ref, iqr_ref, wcol_ref, *rest, n_pages):
    del pt_ref
    page_refs, o_ref = rest[:n_pages], rest[n_pages]
    iqr = iqr_ref[0]
    w = wcol_ref[0] * IDX_SCALE
    for j in range(n_pages):
        ikp = page_refs[j][...].astype(BF16)
        lg = lax.dot_general(iqr, ikp, NT_DIMS, preferred_element_type=F32)
        t = jnp.maximum(lg, 0.0) * w
        sc = t.reshape(SAMPLE_ROWS, N_IDX_HEADS, PAGE_SIZE).sum(axis=1)
        o_ref[0, :, j * PAGE_SIZE:(j + 1) * PAGE_SIZE] = 0.0 + sc


def _sample_scores(page_table, iqr, wcol, pages, layer, n_chunks, n_pages):
    db = iqr.shape[0]
    if layer is not None:
        page_spec = lambda j: pl.BlockSpec(
            (None, None, PAGE_SIZE, IDX_DIM),
            lambda b, c, pt, j=j: (layer, pt[b, c * n_pages + j], 0, 0))
    else:
        page_spec = lambda j: pl.BlockSpec((None, PAGE_SIZE, IDX_DIM), lambda b, c, pt: (b, 0, 0))
    rows = SAMPLE_ROWS * N_IDX_HEADS
    return pl.pallas_call(
        functools.partial(_sample_scores_kernel, n_pages=n_pages),
        out_shape=jax.ShapeDtypeStruct((db, SAMPLE_ROWS, n_chunks * n_pages * PAGE_SIZE), F32),
        grid_spec=pltpu.PrefetchScalarGridSpec(
            num_scalar_prefetch=1,
            grid=(db, n_chunks),
            in_specs=[pl.BlockSpec((1, rows, IDX_DIM), lambda b, c, pt: (b, 0, 0)),
                      pl.BlockSpec((1, rows, 1), lambda b, c, pt: (b, 0, 0))]
                     + [page_spec(j) for j in range(n_pages)],
            out_specs=pl.BlockSpec((1, SAMPLE_ROWS, n_pages * PAGE_SIZE), lambda b, c, pt: (b, 0, c))),
        compiler_params=_cparams(("parallel", "arbitrary"), 32),
        name="sample_idx_scores",
    )(page_table, iqr, wcol, *([pages] * n_pages))


def _sample_select_kernel(sp_ref, sn_ref, bp_ref, bn_ref, keyp_ref, keyn_ref, j_ref, *,
                          k_top, past, rows_per_seq):
    rows = sp_ref.shape[0]
    n = lax.broadcasted_iota(I32, (rows, 1), 0) % rows_per_seq
    colp = lax.broadcasted_iota(I32, (1, past), 1)
    coln = lax.broadcasted_iota(I32, (1, PAGE_SIZE), 1)
    validn = coln <= n
    keyp_ref[...] = _float_key(sp_ref[...])
    keyn_ref[...] = jnp.where(validn, _float_key(sn_ref[...]), jnp.int32(INT_MIN))

    def count_ge(c):
        return _count_lanes(keyp_ref[...] >= c) + _count_lanes(keyn_ref[...] >= c)

    def count_gt_eq(thr):
        kp, kn = keyp_ref[...], keyn_ref[...]
        return (_count_lanes(kp > thr) + _count_lanes(kn > thr),
                _count_lanes(kp == thr) + _count_lanes((kn == thr) & validn))

    def count_eq_below(thr, c):
        return (_count_lanes((keyp_ref[...] == thr) & (colp < c))
                + _count_lanes((keyn_ref[...] == thr) & validn & (coln + past < c)))

    thr = _select_topk(count_ge, count_gt_eq, count_eq_below, (rows, 1), float(k_top),
                       (past + PAGE_SIZE - 1).bit_length(), j_ref)
    j = j_ref[...]
    kp, kn = keyp_ref[...], keyn_ref[...]
    bp_ref[...] = jnp.where((kp > thr) | ((kp == thr) & (colp <= j)), 0.0, MASK_NEG)
    sel_n = ((kn > thr) | ((kn == thr) & (coln + past <= j))) & validn
    bn_ref[...] = jnp.where(sel_n, 0.0, MASK_NEG)


def _sample_select(sc_past, sc_new, k_top, rows_per_seq):
    rows, past = sc_past.shape
    blk = lambda w: pl.BlockSpec((rows, w), lambda i: (0, 0))
    return pl.pallas_call(
        functools.partial(_sample_select_kernel, k_top=k_top, past=past, rows_per_seq=rows_per_seq),
        out_shape=(jax.ShapeDtypeStruct(sc_past.shape, F32), jax.ShapeDtypeStruct(sc_new.shape, F32)),
        grid=(1,),
        in_specs=[blk(past), blk(PAGE_SIZE)],
        out_specs=(blk(past), blk(PAGE_SIZE)),
        scratch_shapes=[pltpu.VMEM((rows, past), I32), pltpu.VMEM((rows, PAGE_SIZE), I32),
                        pltpu.VMEM((rows, 1), I32)],
        compiler_params=_cparams(("arbitrary",), 48),
        name="sample_topk_select",
    )(sc_past, sc_new)


def _sample_attn_kernel(pt_ref, q_ref, bias_ref, knew_ref, vnew_ref, bnew_ref, za_ref, *rest,
                        n_pages, n_chunks):
    del pt_ref
    k_refs, v_refs = rest[:n_pages], rest[n_pages:2 * n_pages]
    o_ref, m_ref, l_ref, acc_ref = rest[2 * n_pages:]
    c = pl.program_id(1)

    @pl.when(c == 0)
    def _():
        m_ref[...] = jnp.full(m_ref.shape, M_INIT, F32)
        l_ref[...] = jnp.zeros(l_ref.shape, F32)
        acc_ref[...] = jnp.zeros(acc_ref.shape, F32)

    def head_rows(page_ref, h):
        return page_ref[pl.ds(h, PAGE_SIZE, stride=N_HEADS), :].astype(BF16)

    def process(k_pages, v_pages, bias):
        s_heads = []
        for h in range(N_HEADS):
            qh = q_ref[:, h * HEAD_DIM:(h + 1) * HEAD_DIM]
            s_heads.append(jnp.concatenate(
                [lax.dot_general(qh, head_rows(kp, h), NT_DIMS, preferred_element_type=F32)
                 for kp in k_pages], axis=1) + bias)
        s = jnp.concatenate(s_heads, axis=0)
        m_old = m_ref[...]
        m_new = jnp.maximum(m_old, jnp.max(s, axis=-1, keepdims=True))
        alpha = jnp.exp2((m_old - m_new) * EXP2_SCALE)
        p = jnp.exp2((s - m_new[:, 0:1]) * EXP2_SCALE)
        l_ref[...] = alpha * l_ref[...] + jnp.sum(p, axis=-1, keepdims=True)
        pv_heads = []
        for h in range(N_HEADS):
            pv = jnp.zeros((SAMPLE_ROWS, HEAD_DIM), F32)
            for j, vp in enumerate(v_pages):
                ph = p[h * SAMPLE_ROWS:(h + 1) * SAMPLE_ROWS, j * PAGE_SIZE:(j + 1) * PAGE_SIZE]
                pv = pv + jnp.dot(ph.astype(BF16), head_rows(vp, h), preferred_element_type=F32)
            pv_heads.append(pv)
        acc_ref[...] = alpha * acc_ref[...] + jnp.concatenate(pv_heads, axis=0)
        m_ref[...] = m_new

    @pl.when(c < n_chunks)
    def _():
        process(k_refs, v_refs, bias_ref[0])

    @pl.when(c == n_chunks)
    def _():
        process([knew_ref], [vnew_ref], bnew_ref[0])
        o = acc_ref[...] / l_ref[...]
        for h in range(N_HEADS):
            sl = slice(h * HEAD_DIM, (h + 1) * HEAD_DIM)
            o_ref[:, sl] = (o[h * SAMPLE_ROWS:(h + 1) * SAMPLE_ROWS] * _silu(za_ref[:, sl])).astype(BF16)


def _sample_attention(page_table, qb, bias_past, k_new, v_new, bias_new, h, cache_k, cache_v, layer):
    db, n_table_pages = page_table.shape
    n_pages = ATTN_PAGES_PER_STEP
    n_chunks = n_table_pages // n_pages
    last = n_table_pages - 1
    page_rows = PAGE_SIZE * N_HEADS
    page_spec = lambda j: pl.BlockSpec(
        (None, None, page_rows, HEAD_DIM),
        lambda b, c, pt, j=j: (layer, pt[b, jnp.minimum(c * n_pages + j, last)], 0, 0))
    new_page = pl.BlockSpec((None, page_rows, HEAD_DIM), lambda b, c, pt: (b, 0, 0))
    state = pltpu.VMEM((N_HEADS * SAMPLE_ROWS, LANES), F32)
    return pl.pallas_call(
        functools.partial(_sample_attn_kernel, n_pages=n_pages, n_chunks=n_chunks),
        out_shape=jax.ShapeDtypeStruct((db * SAMPLE_ROWS, W_SEG), BF16),
        grid_spec=pltpu.PrefetchScalarGridSpec(
            num_scalar_prefetch=1,
            grid=(db, n_chunks + 1),
            in_specs=[pl.BlockSpec((SAMPLE_ROWS, W_SEG), lambda b, c, pt: (b, 0)),
                      pl.BlockSpec((1, SAMPLE_ROWS, n_pages * PAGE_SIZE),
                                   lambda b, c, pt: (b, 0, jnp.minimum(c, n_chunks - 1))),
                      new_page, new_page,
                      pl.BlockSpec((1, SAMPLE_ROWS, PAGE_SIZE), lambda b, c, pt: (b, 0, 0)),
                      pl.BlockSpec((SAMPLE_ROWS, W_SEG), lambda b, c, pt: (b, SEG_ZA))]
                     + [page_spec(j) for j in range(n_pages)] * 2,
            out_specs=pl.BlockSpec((SAMPLE_ROWS, W_SEG), lambda b, c, pt: (b, 0)),
            scratch_shapes=[state, state, state]),
        compiler_params=_cparams(("parallel", "arbitrary"), 48),
        name="sample_sparse_attn",
    )(page_table, qb, bias_past, k_new, v_new, bias_new, h,
      *([cache_k] * n_pages), *([cache_v] * n_pages))


def _merge_kernel(x_ref, pg_ref, ag_ref, gp_ref, ga_ref, wpo_ref, wao_ref, wo_ref, g_ref, b_ref,
                  o_ref, *, alpha):
    p = jnp.dot(pg_ref[...], wpo_ref[...], preferred_element_type=F32)
    a = jnp.dot(ag_ref[...], wao_ref[...], preferred_element_type=F32)
    m = jax.nn.sigmoid(gp_ref[...]) * p + jax.nn.sigmoid(ga_ref[...]) * a
    y = jnp.dot(m.astype(BF16), wo_ref[...], preferred_element_type=F32)
    o_ref[...] = _layer_norm_rows(alpha * x_ref[...] + y, g_ref[...], b_ref[...])


def _merge(x, pg, ag, h, wpo, wao, wo, g, b, tm, alpha):
    m, d = x.shape
    const = lambda a: pl.BlockSpec(a.shape, lambda i: (0, 0), pipeline_mode=pl.Buffered(1))
    return pl.pallas_call(
        functools.partial(_merge_kernel, alpha=alpha),
        out_shape=jax.ShapeDtypeStruct((m, d), F32),
        grid=(m // tm,),
        in_specs=[pl.BlockSpec((tm, d), lambda i: (i, 0)),
                  pl.BlockSpec((tm, W_SEG), lambda i: (i, 0)),
                  pl.BlockSpec((tm, W_SEG), lambda i: (i, 0)),
                  pl.BlockSpec((tm, d), lambda i: (i, SEG_GP)),
                  pl.BlockSpec((tm, d), lambda i: (i, SEG_GA)),
                  const(wpo), const(wao), const(wo),
                  pl.BlockSpec((1, d), lambda i: (0, 0)),
                  pl.BlockSpec((1, d), lambda i: (0, 0))],
        out_specs=pl.BlockSpec((tm, d), lambda i: (i, 0)),
        compiler_params=_cparams(("parallel",), 56),
        name="merge_out_ln",
    )(x, pg, ag, h, h, wpo, wao, wo, g.reshape(1, d), b.reshape(1, d))


def _pack_w_in(w):
    d = w.shape[0]
    n_main = 7 * W_SEG
    n_small = IDX_DIM + N_IDX_HEADS
    pad = jnp.zeros((d, SMALL_W - n_small), w.dtype)
    gates = w[:, n_main + n_small:]
    return jnp.concatenate([gates, w[:, :n_main], w[:, n_main:n_main + n_small], pad],
                           axis=1).astype(BF16)


def kernel(x_prompt, x_sample, cache_k, cache_v, cache_idx_k, state_pool, page_table, ln_emb_g, ln_emb_b, w_in, w_pool_mix, pool_scale, w_pool_out, w_attn_out, w_o, ln_g, ln_b):
    batch, seq, d_model = x_prompt.shape
    dec_batch, dec_seq, _ = x_sample.shape
    depth = w_in.shape[0]
    n_table_pages = page_table.shape[1]
    past = n_table_pages * PAGE_SIZE
    alpha = (2 * depth) ** 0.25
    assert w_in.shape[2] == H_WIDTH - (SMALL_W - IDX_DIM - N_IDX_HEADS)
    assert seq % 512 == 0 and dec_seq <= SAMPLE_ROWS
    assert n_table_pages % SCORE_PAGES_PER_STEP == 0 and n_table_pages % ATTN_PAGES_PER_STEP == 0
    mp = batch * seq
    ms = dec_batch * SAMPLE_ROWS

    xp = _layer_norm(x_prompt.reshape(mp, d_model), ln_emb_g, ln_emb_b, 256)
    xs_in = jnp.pad(x_sample, ((0, 0), (0, SAMPLE_ROWS - dec_seq), (0, 0))).reshape(ms, d_model)
    xs = _layer_norm(xs_in, ln_emb_g, ln_emb_b, ms)

    tab_p = _rope_tables(jnp.arange(seq, dtype=I32))
    tab_s = _rope_tables(jnp.tile(past + jnp.arange(SAMPLE_ROWS, dtype=I32), dec_batch))
    k_top_s = min(TOPK_MAX, (past + dec_seq) // 4)
    real = lambda a, *tail: a.reshape((dec_batch, SAMPLE_ROWS) + tail)[:, :dec_seq]
    cache_k2 = cache_k.reshape(depth, cache_k.shape[1], PAGE_SIZE * N_HEADS, HEAD_DIM)
    cache_v2 = cache_v.reshape(depth, cache_v.shape[1], PAGE_SIZE * N_HEADS, HEAD_DIM)

    kp, vp, ikp, plp, kss, vss, iks, pls = [], [], [], [], [], [], [], []
    for l in range(depth):
        w_all = _pack_w_in(w_in[l])
        wmix = w_pool_mix[l].astype(BF16)
        scale = pool_scale[l].reshape(1, -1)
        wpo, wao, wo = w_pool_out[l].astype(BF16), w_attn_out[l].astype(BF16), w_o[l].astype(BF16)

        h = _project(xp, w_all, 512, 1280)
        k_f, v_f, qb, kb, vt, iqb, ik_f, ike, iko = _rope(h, tab_p, K_TILE, seq // K_TILE)
        h3 = h.reshape(batch, seq, H_WIDTH)
        pg = _pool(h3, jnp.zeros((batch, HIST_ROWS, W_SEG), F32), wmix, scale, 256, 0)
        ag = _prompt_attention(qb, kb, vt, iqb, ike, iko, h, batch, seq)
        kp.append(k_f.reshape(batch, seq, N_HEADS, HEAD_DIM))
        vp.append(v_f.reshape(batch, seq, N_HEADS, HEAD_DIM))
        ikp.append(ik_f[:, :IDX_DIM].reshape(batch, seq, IDX_DIM))
        plp.append(h3[:, seq - POOL_BUF:, SEG_UP * W_SEG:(SEG_UP + 1) * W_SEG])
        xp = _merge(xp, pg.reshape(mp, W_SEG), ag, h, wpo, wao, wo, ln_g[l], ln_b[l], 256, alpha)

        hs = _project(xs, w_all, ms, 1280)
        k_f, v_f, qb, _, _, iqb, ik_f, _, _ = _rope(hs, tab_s, ms, 1)
        hs3 = hs.reshape(dec_batch, SAMPLE_ROWS, H_WIDTH)
        hist = jnp.pad(state_pool[l], ((0, 0), (HIST_ROWS - POOL_BUF, 0), (0, 0)))
        pg = _pool(hs3, hist, wmix, scale, SAMPLE_ROWS, past)
        iqr = iqb.reshape(dec_batch, SAMPLE_ROWS * N_IDX_HEADS, IDX_DIM)
        wcol = ik_f[:, IDX_DIM:IDX_DIM + N_IDX_HEADS].reshape(dec_batch, SAMPLE_ROWS * N_IDX_HEADS, 1)
        pad_page = lambda a: jnp.pad(a, ((0, 0), (0, PAGE_SIZE - SAMPLE_ROWS)) + ((0, 0),) * (a.ndim - 2))
        ik_new = pad_page(ik_f[:, :IDX_DIM].reshape(dec_batch, SAMPLE_ROWS, IDX_DIM))
        sc_past = _sample_scores(page_table, iqr, wcol, cache_idx_k, l,
                                 n_table_pages // SCORE_PAGES_PER_STEP, SCORE_PAGES_PER_STEP)
        sc_new = _sample_scores(page_table, iqr, wcol, ik_new, None, 1, 1)
        bias_past, bias_new = _sample_select(
            sc_past[:, :dec_seq].reshape(dec_batch * dec_seq, past),
            sc_new[:, :dec_seq].reshape(dec_batch * dec_seq, PAGE_SIZE), k_top_s, dec_seq)
        pad_rows = lambda a, w: jnp.pad(a.reshape(dec_batch, dec_seq, w),
                                        ((0, 0), (0, SAMPLE_ROWS - dec_seq), (0, 0)))
        new_page = lambda a: pad_page(a.reshape(dec_batch, SAMPLE_ROWS, N_HEADS, HEAD_DIM)).reshape(
            dec_batch, PAGE_SIZE * N_HEADS, HEAD_DIM)
        ag = _sample_attention(page_table, qb, pad_rows(bias_past, past), new_page(k_f), new_page(v_f),
                               pad_rows(bias_new, PAGE_SIZE), hs, cache_k2, cache_v2, l)
        kss.append(real(k_f, N_HEADS, HEAD_DIM))
        vss.append(real(v_f, N_HEADS, HEAD_DIM))
        iks.append(real(ik_f, LANES)[:, :, :IDX_DIM])
        us = hs3[:, :dec_seq, SEG_UP * W_SEG:(SEG_UP + 1) * W_SEG]
        pls.append(jnp.concatenate([state_pool[l], us], axis=1)[:, -POOL_BUF:])
        xs = _merge(xs, pg.reshape(ms, W_SEG), ag, hs, wpo, wao, wo, ln_g[l], ln_b[l], ms, alpha)

    y_sample = xs.reshape(dec_batch, SAMPLE_ROWS, d_model)[:, :dec_seq]
    return (xp.reshape(batch, seq, d_model), y_sample, jnp.stack(kp), jnp.stack(vp), jnp.stack(ikp),
            jnp.stack(plp), jnp.stack(kss), jnp.stack(vss), jnp.stack(iks), jnp.stack(pls))
```

```python
import functools
import math

import jax
import jax.numpy as jnp
from jax import lax
from jax.experimental import pallas as pl
from jax.experimental.pallas import tpu as pltpu

F32 = jnp.float32
BF16 = jnp.bfloat16
I32 = jnp.int32

N_HEADS = 8
HEAD_DIM = 128
N_IDX_HEADS = 16
IDX_DIM = 64
IDX_SCALE = (N_IDX_HEADS * IDX_DIM) ** -0.5
POOL_WINDOWS = (2, 4, 8, 16)
POOL_GROUP = 256
POOL_BUF = 15
TOPK_MAX = 256
ROPE_THETA = 10000.0
LN_EPS = 1e-5
PAGE_SIZE = 128

LANES = 128
HIST_ROWS = 16
SAMPLE_ROWS = 16
SMALL_W = 256
Q_TILE = 256
K_TILE = 256
SCORE_ROWS = 128
ACC_ROWS = 64
SMALL_ACC_ROWS = 8
SCORE_PAGES_PER_STEP = 16
ATTN_PAGES_PER_STEP = 8

W_SEG = 1024
SEG_UP, SEG_ZP, SEG_Q, SEG_K, SEG_V, SEG_ZA, SEG_IQ = 0, 1, 2, 3, 4, 5, 6
N_MAIN = 7 * W_SEG
SEG_SMALL = N_MAIN // SMALL_W
GATE_SHIFT = IDX_DIM + N_IDX_HEADS
GATE_W = 2048
H_WIDTH = N_MAIN + GATE_SHIFT + 2 * GATE_W

INT_MIN = -2 ** 31
NT_DIMS = (((1,), (1,)), ((), ()))
MASK_NEG = float("-inf")
M_INIT = -1e30
EXP2_SCALE = HEAD_DIM ** -0.5 * math.log2(math.e)


def _cparams(sem, vmem_mb):
    return pltpu.CompilerParams(dimension_semantics=sem, vmem_limit_bytes=vmem_mb * 1024 * 1024)


def _silu(z):
    return z * jax.nn.sigmoid(z)


def _layer_norm_rows(x, g, b):
    mu = jnp.mean(x, axis=-1, keepdims=True)
    xc = x - mu
    var = jnp.mean(xc * xc, axis=-1, keepdims=True)
    return xc * lax.rsqrt(var + LN_EPS) * g + b


def _ln_kernel(x_ref, g_ref, b_ref, o_ref):
    o_ref[...] = _layer_norm_rows(x_ref[...], g_ref[...], b_ref[...])


def _layer_norm(x, g, b, tm):
    m, d = x.shape
    return pl.pallas_call(
        _ln_kernel,
        out_shape=jax.ShapeDtypeStruct((m, d), F32),
        grid=(m // tm,),
        in_specs=[pl.BlockSpec((tm, d), lambda i: (i, 0)),
                  pl.BlockSpec((1, d), lambda i: (0, 0)),
                  pl.BlockSpec((1, d), lambda i: (0, 0))],
        out_specs=pl.BlockSpec((tm, d), lambda i: (i, 0)),
        compiler_params=_cparams(("parallel",), 32),
        name="ln_embed",
    )(x, g.reshape(1, d), b.reshape(1, d))


def _proj_kernel(x_ref, w_ref, o_ref, xb_ref):
    @pl.when(pl.program_id(1) == 0)
    def _():
        xb_ref[...] = x_ref[...].astype(BF16)

    o_ref[...] = lax.dot_general(xb_ref[...], w_ref[...].astype(BF16), NT_DIMS,
                                 preferred_element_type=F32)


def _project(x, w_t, layer, tm, tn):
    m, d = x.shape
    n = w_t.shape[1]
    return pl.pallas_call(
        _proj_kernel,
        out_shape=jax.ShapeDtypeStruct((m, n), F32),
        grid=(m // tm, pl.cdiv(n, tn)),
        in_specs=[pl.BlockSpec((tm, d), lambda i, j: (i, 0)),
                  pl.BlockSpec((None, tn, d), lambda i, j: (layer, j, 0))],
        out_specs=pl.BlockSpec((tm, tn), lambda i, j: (i, j)),
        scratch_shapes=[pltpu.VMEM((tm, d), BF16)],
        compiler_params=_cparams(("parallel", "arbitrary"), 48),
        name="in_proj",
    )(x, w_t)


def _rope_kernel(q_ref, k_ref, v_ref, iq_ref, sm_ref, tab_ref,
                 ko_ref, vo_ref, qb_ref, kb_ref, vt_ref, iqb_ref, iks_ref, ike_ref, iko_ref):
    cos_h, sin_h = tab_ref[0], tab_ref[1]
    cos_i, sin_i = tab_ref[2], tab_ref[3]
    cos_s, sin_s = tab_ref[4], tab_ref[5]
    lane = lax.broadcasted_iota(I32, cos_h.shape, 1)
    low_half = (lane % IDX_DIM) < (IDX_DIM // 2)

    def rope_idx(x, cos, sin):
        partner = jnp.where(low_half, pltpu.roll(x, LANES - IDX_DIM // 2, 1),
                            pltpu.roll(x, IDX_DIM // 2, 1))
        return x * cos + partner * sin

    for h in range(N_HEADS):
        sl = slice(h * HEAD_DIM, (h + 1) * HEAD_DIM)
        x = q_ref[:, sl]
        qb_ref[:, sl] = (x * cos_h + pltpu.roll(x, HEAD_DIM // 2, 1) * sin_h).astype(BF16)
        x = k_ref[:, sl]
        kr = x * cos_h + pltpu.roll(x, HEAD_DIM // 2, 1) * sin_h
        ko_ref[:, sl] = kr
        kb_ref[:, sl] = kr.astype(BF16)
        iqb_ref[:, sl] = rope_idx(iq_ref[:, sl], cos_i, sin_i).astype(BF16)
        v = v_ref[:, sl]
        vo_ref[:, sl] = v
        vt_ref[0, sl, :] = v.T.astype(BF16)
    xs = rope_idx(sm_ref[:, 0:LANES], cos_s, sin_s)
    iks_ref[...] = xs
    ik_only = jnp.where(lane < IDX_DIM, xs, 0.0)
    ike_ref[...] = ik_only.astype(BF16)
    iko_ref[...] = pltpu.roll(ik_only, IDX_DIM, 1).astype(BF16)


def _rope(h, tab, tm, tab_blocks):
    m = h.shape[0]
    seg = lambda s: pl.BlockSpec((tm, W_SEG), lambda i, s=s: (i, s))
    row = lambda w: pl.BlockSpec((tm, w), lambda i: (i, 0))
    f32o = lambda w: jax.ShapeDtypeStruct((m, w), F32)
    b16o = lambda w: jax.ShapeDtypeStruct((m, w), BF16)
    return pl.pallas_call(
        _rope_kernel,
        out_shape=(f32o(W_SEG), f32o(W_SEG), b16o(W_SEG), b16o(W_SEG),
                   jax.ShapeDtypeStruct((m // tm, W_SEG, tm), BF16), b16o(W_SEG),
                   f32o(LANES), b16o(LANES), b16o(LANES)),
        grid=(m // tm,),
        in_specs=[seg(SEG_Q), seg(SEG_K), seg(SEG_V), seg(SEG_IQ),
                  pl.BlockSpec((tm, SMALL_W), lambda i: (i, SEG_SMALL)),
                  pl.BlockSpec((6, tm, LANES), lambda i: (0, i % tab_blocks, 0))],
        out_specs=(row(W_SEG), row(W_SEG), row(W_SEG), row(W_SEG),
                   pl.BlockSpec((1, W_SEG, tm), lambda i: (i, 0, 0)), row(W_SEG),
                   row(LANES), row(LANES), row(LANES)),
        compiler_params=_cparams(("parallel",), 48),
        name="rope_split",
    )(h, h, h, h, h, tab)


def _rope_tables(pos):
    def cs(half):
        inv = ROPE_THETA ** (-jnp.arange(half, dtype=F32) / half)
        ang = pos.astype(F32)[:, None] * inv[None, :]
        return jnp.cos(ang), jnp.sin(ang)
    ch, sh = cs(HEAD_DIM // 2)
    ci, si = cs(IDX_DIM // 2)
    n = pos.shape[0]
    one = jnp.ones((n, LANES - IDX_DIM), F32)
    return jnp.stack([
        jnp.concatenate([ch, ch], axis=1), jnp.concatenate([-sh, sh], axis=1),
        jnp.concatenate([ci, ci, ci, ci], axis=1), jnp.concatenate([-si, si, -si, si], axis=1),
        jnp.concatenate([ci, ci, one], axis=1), jnp.concatenate([-si, si, 0.0 * one], axis=1)])


def _pool_kernel(up_ref, zp_ref, hist_ref, wmix_ref, scale_ref, o_ref, ext_ref, *, tm, pos0):
    t = pl.program_id(1)

    @pl.when(t == 0)
    def _():
        ext_ref[0:HIST_ROWS, :] = hist_ref[0]

    @pl.when(t > 0)
    def _():
        ext_ref[0:HIST_ROWS, :] = ext_ref[tm:tm + HIST_ROWS, :]

    u = up_ref[0]
    ext_ref[HIST_ROWS:HIST_ROWS + tm, :] = u
    pos = pos0 + t * tm + lax.broadcasted_iota(I32, (tm, 1), 0)
    for g, w in enumerate(POOL_WINDOWS):
        sl = slice(g * POOL_GROUP, (g + 1) * POOL_GROUP)
        acc = u[:, sl]
        for j in range(1, w):
            acc = acc + ext_ref[HIST_ROWS - j:HIST_ROWS - j + tm, sl]
        cnt = jnp.minimum(pos + 1, w).astype(F32)
        d = acc / cnt - u[:, sl]
        mixed = jnp.dot(d.astype(BF16), wmix_ref[g], preferred_element_type=F32)
        o_ref[0, :, sl] = (mixed * scale_ref[:, sl] * _silu(zp_ref[0, :, sl])).astype(BF16)


def _pool(h3, hist, wmix, scale, tm, pos0):
    b, t, _ = h3.shape
    width = scale.shape[1]
    return pl.pallas_call(
        functools.partial(_pool_kernel, tm=tm, pos0=pos0),
        out_shape=jax.ShapeDtypeStruct((b, t, width), BF16),
        grid=(b, t // tm),
        in_specs=[pl.BlockSpec((1, tm, W_SEG), lambda i, j: (i, j, SEG_UP)),
                  pl.BlockSpec((1, tm, W_SEG), lambda i, j: (i, j, SEG_ZP)),
                  pl.BlockSpec((1, HIST_ROWS, width), lambda i, j: (i, 0, 0)),
                  pl.BlockSpec(wmix.shape, lambda i, j: (0, 0, 0)),
                  pl.BlockSpec((1, width), lambda i, j: (0, 0))],
        out_specs=pl.BlockSpec((1, tm, width), lambda i, j: (i, j, 0)),
        scratch_shapes=[pltpu.VMEM((HIST_ROWS + tm, width), F32)],
        compiler_params=_cparams(("parallel", "arbitrary"), 32),
        name="pool_mixer",
    )(h3, h3, hist, wmix, scale)


def _float_key(x):
    bits = pltpu.bitcast(x, I32)
    return bits ^ ((bits >> 31) & jnp.int32(0x7FFFFFFF))


def _select_topk(count_ge, count_gt_eq, count_eq_below, shape, k_top, idx_bits, j_ref):
    def bit_body(i, t_u):
        c_u = t_u | (jnp.int32(1) << (31 - i))
        cnt = count_ge(c_u ^ jnp.int32(INT_MIN))
        return jnp.where(cnt >= k_top, c_u, t_u)

    thr = lax.fori_loop(0, 32, bit_body, jnp.zeros(shape, I32)) ^ jnp.int32(INT_MIN)
    n_gt, n_eq = count_gt_eq(thr)
    need = k_top - n_gt
    j_ref[...] = jnp.full(shape, (1 << idx_bits) - 1, I32)

    @pl.when(jnp.max(n_eq - need) > 0.0)
    def _():
        def idx_body(i, j):
            c = j | (jnp.int32(1) << (idx_bits - 1 - i))
            return jnp.where(count_eq_below(thr, c) < need, c, j)
        j_ref[...] = lax.fori_loop(0, idx_bits, idx_body, jnp.zeros(shape, I32))

    return thr


def _count_lanes(mask):
    x = jnp.where(mask, 1.0, 0.0)
    w = x.shape[1]
    while w > 8 * LANES and w % (2 * LANES) == 0:
        w //= 2
        x = x[:, :w] + x[:, w:]
    return jnp.sum(x, axis=1, keepdims=True)


def _prompt_attn_kernel(q_ref, k_ref, vt_ref, iq_ref, ike_ref, iko_ref, sm_ref, za_ref, o_ref,
                        key_ref, bias_ref, s_ref, o_acc_ref, j_ref, *, k_top, idx_bits):
    i = pl.program_id(1)
    n_chunks = i + 1
    qpos = i * Q_TILE + lax.broadcasted_iota(I32, (1, Q_TILE), 1)
    sm_t = sm_ref[:, 0:LANES].T
    w_t = sm_t[IDX_DIM:IDX_DIM + N_IDX_HEADS, :] * IDX_SCALE

    def score_chunk(c, carry):
        r0 = pl.multiple_of(c * SCORE_ROWS, SCORE_ROWS)
        ike = ike_ref[pl.ds(r0, SCORE_ROWS), :]
        iko = iko_ref[pl.ds(r0, SCORE_ROWS), :]
        acc = jnp.zeros((SCORE_ROWS, Q_TILE), F32)
        for p in range(N_IDX_HEADS // 2):
            rhs = iq_ref[:, p * LANES:(p + 1) * LANES]
            lg_e = lax.dot_general(ike, rhs, NT_DIMS, preferred_element_type=F32)
            lg_o = lax.dot_general(iko, rhs, NT_DIMS, preferred_element_type=F32)
            acc = acc + jnp.maximum(lg_e, 0.0) * w_t[2 * p:2 * p + 1, :]
            acc = acc + jnp.maximum(lg_o, 0.0) * w_t[2 * p + 1:2 * p + 2, :]
        kidx = r0 + lax.broadcasted_iota(I32, (SCORE_ROWS, 1), 0)
        key_ref[pl.ds(r0, SCORE_ROWS), :] = jnp.where(kidx <= qpos, _float_key(acc),
                                                      jnp.int32(INT_MIN))
        return carry

    lax.fori_loop(0, n_chunks * (K_TILE // SCORE_ROWS), score_chunk, 0)

    def fold(x, op):
        return op(x.reshape(K_TILE // ACC_ROWS, ACC_ROWS, Q_TILE), axis=0)

    def count_where(pred):
        def body(c, acc):
            r0 = pl.multiple_of(c * K_TILE, K_TILE)
            kidx = r0 + lax.broadcasted_iota(I32, (K_TILE, 1), 0)
            x = jnp.where(pred(key_ref[pl.ds(r0, K_TILE), :], kidx), 1.0, 0.0)
            return acc + fold(x, jnp.sum)
        acc = lax.fori_loop(0, n_chunks, body, jnp.zeros((ACC_ROWS, Q_TILE), F32))
        return jnp.sum(acc, axis=0, keepdims=True)

    def count_ge(c):
        return count_where(lambda keys, kidx: keys >= c)

    def count_gt_eq(thr):
        return (count_where(lambda keys, kidx: keys > thr),
                count_where(lambda keys, kidx: (keys == thr) & (kidx <= qpos)))

    def count_eq_below(thr, c):
        return count_where(lambda keys, kidx: (keys == thr) & (kidx <= qpos) & (kidx < c))

    thr = _select_topk(count_ge, count_gt_eq, count_eq_below, (1, Q_TILE), float(k_top),
                       idx_bits, j_ref)
    j_sel = j_ref[...]

    def bias_chunk(c, carry):
        r0 = pl.multiple_of(c * K_TILE, K_TILE)
        kidx = r0 + lax.broadcasted_iota(I32, (K_TILE, 1), 0)
        keys = key_ref[pl.ds(r0, K_TILE), :]
        sel = ((keys > thr) | ((keys == thr) & (kidx <= j_sel))) & (kidx <= qpos)
        bias_ref[pl.ds(r0, K_TILE), :] = jnp.where(sel, 0.0, MASK_NEG)
        return carry

    lax.fori_loop(0, n_chunks, bias_chunk, 0)

    heads = [slice(h * HEAD_DIM, (h + 1) * HEAD_DIM) for h in range(N_HEADS)]

    def fold_small(x, op):
        return op(x.reshape(K_TILE // SMALL_ACC_ROWS, SMALL_ACC_ROWS, Q_TILE), axis=0)

    def raw_scores(c, m_accs):
        r0 = pl.multiple_of(c * K_TILE, K_TILE)
        out = []
        for h, sl in enumerate(heads):
            s = lax.dot_general(k_ref[pl.ds(r0, K_TILE), sl], q_ref[:, sl], NT_DIMS,
                                preferred_element_type=F32)
            s = s + bias_ref[pl.ds(r0, K_TILE), :]
            s_ref[h, pl.ds(r0, K_TILE), :] = s
            out.append(jnp.maximum(m_accs[h], fold_small(s, jnp.max)))
        return tuple(out)

    m_accs = lax.fori_loop(
        0, n_chunks, raw_scores,
        tuple(jnp.full((SMALL_ACC_ROWS, Q_TILE), MASK_NEG, F32) for _ in heads))
    m_heads = [jnp.max(a, axis=0, keepdims=True) for a in m_accs]
    o_acc_ref[...] = jnp.zeros(o_acc_ref.shape, F32)

    def weighted_values(c, l_accs):
        r0 = pl.multiple_of(c * K_TILE, K_TILE)
        out = []
        for h, sl in enumerate(heads):
            p = jnp.exp2((s_ref[h, pl.ds(r0, K_TILE), :] - m_heads[h]) * EXP2_SCALE)
            o_acc_ref[h] += jnp.dot(vt_ref[c, sl, :], p.astype(BF16), preferred_element_type=F32)
            out.append(l_accs[h] + fold_small(p, jnp.sum))
        return tuple(out)

    l_accs = lax.fori_loop(
        0, n_chunks, weighted_values,
        tuple(jnp.zeros((SMALL_ACC_ROWS, Q_TILE), F32) for _ in heads))
    for h, sl in enumerate(heads):
        o_t = o_acc_ref[h] / jnp.sum(l_accs[h], axis=0, keepdims=True)
        o_ref[:, sl] = (o_t.T * _silu(za_ref[:, sl])).astype(BF16)


def _prompt_attention(qb, kb, vt, iqb, ike, iko, h, batch, seq):
    m = qb.shape[0]
    nq = seq // Q_TILE
    k_top = min(TOPK_MAX, seq // 4)
    qrow = lambda w: pl.BlockSpec((Q_TILE, w), lambda b, i: (b * nq + i, 0))
    kv = lambda w: pl.BlockSpec((seq, w), lambda b, i: (b, 0), pipeline_mode=pl.Buffered(1))
    return pl.pallas_call(
        functools.partial(_prompt_attn_kernel, k_top=k_top, idx_bits=(seq - 1).bit_length()),
        out_shape=jax.ShapeDtypeStruct((m, W_SEG), BF16),
        grid=(batch, nq),
        in_specs=[qrow(W_SEG), kv(W_SEG),
                  pl.BlockSpec((seq // K_TILE, W_SEG, K_TILE), lambda b, i: (b, 0, 0),
                               pipeline_mode=pl.Buffered(1)),
                  qrow(W_SEG), kv(LANES), kv(LANES),
                  pl.BlockSpec((Q_TILE, SMALL_W), lambda b, i: (b * nq + i, SEG_SMALL)),
                  pl.BlockSpec((Q_TILE, W_SEG), lambda b, i: (b * nq + i, SEG_ZA))],
        out_specs=qrow(W_SEG),
        scratch_shapes=[pltpu.VMEM((seq, Q_TILE), I32), pltpu.VMEM((seq, Q_TILE), F32),
                        pltpu.VMEM((N_HEADS, seq, Q_TILE), F32),
                        pltpu.VMEM((N_HEADS, HEAD_DIM, Q_TILE), F32),
                        pltpu.VMEM((1, Q_TILE), I32)],
        compiler_params=_cparams(("parallel", "arbitrary"), 56),
        name="prompt_sparse_attn",
    )(qb, kb, vt, iqb, ike, iko, h, h)


def _sample_scores_kernel(pt_ref, iqr_ref, wcol_ref, *rest, n_pages):
    del pt_ref
    page_refs, o_ref = rest[:n_pages], rest[n_pages]
    iqr = iqr_ref[0]
    w = wcol_ref[0] * IDX_SCALE
    for j in range(n_pages):
        ikp_t = page_refs[j][...].astype(BF16)
        lg = jnp.dot(iqr, ikp_t, preferred_element_type=F32)
        t = jnp.maximum(lg, 0.0) * w
        sc = t.reshape(SAMPLE_ROWS, N_IDX_HEADS, PAGE_SIZE).sum(axis=1)
        o_ref[0, :, j * PAGE_SIZE:(j + 1) * PAGE_SIZE] = 0.0 + sc


def _sample_scores(page_table, iqr, wcol, pages, layer, n_chunks, n_pages):
    db = iqr.shape[0]
    if layer is not None:
        page_spec = lambda j: pl.BlockSpec(
            (None, None, IDX_DIM, PAGE_SIZE),
            lambda b, c, pt, j=j: (layer, pt[b, c * n_pages + j], 0, 0))
    else:
        page_spec = lambda j: pl.BlockSpec((None, IDX_DIM, PAGE_SIZE), lambda b, c, pt: (b, 0, 0))
    rows = SAMPLE_ROWS * N_IDX_HEADS
    return pl.pallas_call(
        functools.partial(_sample_scores_kernel, n_pages=n_pages),
        out_shape=jax.ShapeDtypeStruct((db, SAMPLE_ROWS, n_chunks * n_pages * PAGE_SIZE), F32),
        grid_spec=pltpu.PrefetchScalarGridSpec(
            num_scalar_prefetch=1,
            grid=(db, n_chunks),
            in_specs=[pl.BlockSpec((1, rows, IDX_DIM), lambda b, c, pt: (b, 0, 0)),
                      pl.BlockSpec((1, rows, 1), lambda b, c, pt: (b, 0, 0))]
                     + [page_spec(j) for j in range(n_pages)],
            out_specs=pl.BlockSpec((1, SAMPLE_ROWS, n_pages * PAGE_SIZE), lambda b, c, pt: (b, 0, c))),
        compiler_params=_cparams(("parallel", "arbitrary"), 32),
        name="sample_idx_scores",
    )(page_table, iqr, wcol, *([pages] * n_pages))


def _sample_select_kernel(sp_ref, sn_ref, bp_ref, bn_ref, keyp_ref, keyn_ref, j_ref, *,
                          k_top, past, rows_per_seq):
    rows = sp_ref.shape[0]
    n = lax.broadcasted_iota(I32, (rows, 1), 0) % rows_per_seq
    colp = lax.broadcasted_iota(I32, (1, past), 1)
    coln = lax.broadcasted_iota(I32, (1, PAGE_SIZE), 1)
    validn = coln <= n
    keyp_ref[...] = _float_key(sp_ref[...])
    keyn_ref[...] = jnp.where(validn, _float_key(sn_ref[...]), jnp.int32(INT_MIN))

    def count_ge(c):
        return _count_lanes(keyp_ref[...] >= c) + _count_lanes(keyn_ref[...] >= c)

    def count_gt_eq(thr):
        kp, kn = keyp_ref[...], keyn_ref[...]
        return (_count_lanes(kp > thr) + _count_lanes(kn > thr),
                _count_lanes(kp == thr) + _count_lanes((kn == thr) & validn))

    def count_eq_below(thr, c):
        return (_count_lanes((keyp_ref[...] == thr) & (colp < c))
                + _count_lanes((keyn_ref[...] == thr) & validn & (coln + past < c)))

    thr = _select_topk(count_ge, count_gt_eq, count_eq_below, (rows, 1), float(k_top),
                       (past + PAGE_SIZE - 1).bit_length(), j_ref)
    j = j_ref[...]
    kp, kn = keyp_ref[...], keyn_ref[...]
    bp_ref[...] = jnp.where((kp > thr) | ((kp == thr) & (colp <= j)), 0.0, MASK_NEG)
    sel_n = ((kn > thr) | ((kn == thr) & (coln + past <= j))) & validn
    bn_ref[...] = jnp.where(sel_n, 0.0, MASK_NEG)


def _sample_select(sc_past, sc_new, k_top, rows_per_seq):
    rows, past = sc_past.shape
    blk = lambda w: pl.BlockSpec((rows, w), lambda i: (0, 0))
    return pl.pallas_call(
        functools.partial(_sample_select_kernel, k_top=k_top, past=past, rows_per_seq=rows_per_seq),
        out_shape=(jax.ShapeDtypeStruct(sc_past.shape, F32), jax.ShapeDtypeStruct(sc_new.shape, F32)),
        grid=(1,),
        in_specs=[blk(past), blk(PAGE_SIZE)],
        out_specs=(blk(past), blk(PAGE_SIZE)),
        scratch_shapes=[pltpu.VMEM((rows, past), I32), pltpu.VMEM((rows, PAGE_SIZE), I32),
                        pltpu.VMEM((rows, 1), I32)],
        compiler_params=_cparams(("arbitrary",), 48),
        name="sample_topk_select",
    )(sc_past, sc_new)


def _sample_attn_kernel(pt_ref, q_ref, bias_ref, knew_ref, vnew_ref, bnew_ref, za_ref, *rest,
                        n_pages, n_chunks):
    del pt_ref
    k_refs, v_refs = rest[:n_pages], rest[n_pages:2 * n_pages]
    o_ref, m_ref, l_ref, acc_ref = rest[2 * n_pages:]
    c = pl.program_id(1)

    @pl.when(c == 0)
    def _():
        m_ref[...] = jnp.full(m_ref.shape, M_INIT, F32)
        l_ref[...] = jnp.zeros(l_ref.shape, F32)
        acc_ref[...] = jnp.zeros(acc_ref.shape, F32)

    def head_rows(page_ref, h):
        return page_ref[pl.ds(h, PAGE_SIZE, stride=N_HEADS), :].astype(BF16)

    def process(k_pages, v_pages, bias):
        s_heads = []
        for h in range(N_HEADS):
            qh = q_ref[:, h * HEAD_DIM:(h + 1) * HEAD_DIM]
            s_heads.append(jnp.concatenate(
                [lax.dot_general(qh, head_rows(kp, h), NT_DIMS, preferred_element_type=F32)
                 for kp in k_pages], axis=1) + bias)
        s = jnp.concatenate(s_heads, axis=0)
        m_old = m_ref[...]
        m_new = jnp.maximum(m_old, jnp.max(s, axis=-1, keepdims=True))
        alpha = jnp.exp2((m_old - m_new) * EXP2_SCALE)
        p = jnp.exp2((s - m_new[:, 0:1]) * EXP2_SCALE)
        l_ref[...] = alpha * l_ref[...] + jnp.sum(p, axis=-1, keepdims=True)
        pv_heads = []
        for h in range(N_HEADS):
            pv = jnp.zeros((SAMPLE_ROWS, HEAD_DIM), F32)
            for j, vp in enumerate(v_pages):
                ph = p[h * SAMPLE_ROWS:(h + 1) * SAMPLE_ROWS, j * PAGE_SIZE:(j + 1) * PAGE_SIZE]
                pv = pv + jnp.dot(ph.astype(BF16), head_rows(vp, h), preferred_element_type=F32)
            pv_heads.append(pv)
        acc_ref[...] = alpha * acc_ref[...] + jnp.concatenate(pv_heads, axis=0)
        m_ref[...] = m_new

    @pl.when(c < n_chunks)
    def _():
        process(k_refs, v_refs, bias_ref[0])

    @pl.when(c == n_chunks)
    def _():
        process([knew_ref], [vnew_ref], bnew_ref[0])
        o = acc_ref[...] / l_ref[...]
        for h in range(N_HEADS):
            sl = slice(h * HEAD_DIM, (h + 1) * HEAD_DIM)
            o_ref[:, sl] = (o[h * SAMPLE_ROWS:(h + 1) * SAMPLE_ROWS] * _silu(za_ref[:, sl])).astype(BF16)


def _sample_attention(page_table, qb, bias_past, k_new, v_new, bias_new, h, cache_k, cache_v, layer):
    db, n_table_pages = page_table.shape
    n_pages = ATTN_PAGES_PER_STEP
    n_chunks = n_table_pages // n_pages
    last = n_table_pages - 1
    page_rows = PAGE_SIZE * N_HEADS
    page_spec = lambda j: pl.BlockSpec(
        (None, None, page_rows, HEAD_DIM),
        lambda b, c, pt, j=j: (layer, pt[b, jnp.minimum(c * n_pages + j, last)], 0, 0))
    new_page = pl.BlockSpec((None, page_rows, HEAD_DIM), lambda b, c, pt: (b, 0, 0))
    state = pltpu.VMEM((N_HEADS * SAMPLE_ROWS, LANES), F32)
    return pl.pallas_call(
        functools.partial(_sample_attn_kernel, n_pages=n_pages, n_chunks=n_chunks),
        out_shape=jax.ShapeDtypeStruct((db * SAMPLE_ROWS, W_SEG), BF16),
        grid_spec=pltpu.PrefetchScalarGridSpec(
            num_scalar_prefetch=1,
            grid=(db, n_chunks + 1),
            in_specs=[pl.BlockSpec((SAMPLE_ROWS, W_SEG), lambda b, c, pt: (b, 0)),
                      pl.BlockSpec((1, SAMPLE_ROWS, n_pages * PAGE_SIZE),
                                   lambda b, c, pt: (b, 0, jnp.minimum(c, n_chunks - 1))),
                      new_page, new_page,
                      pl.BlockSpec((1, SAMPLE_ROWS, PAGE_SIZE), lambda b, c, pt: (b, 0, 0)),
                      pl.BlockSpec((SAMPLE_ROWS, W_SEG), lambda b, c, pt: (b, SEG_ZA))]
                     + [page_spec(j) for j in range(n_pages)] * 2,
            out_specs=pl.BlockSpec((SAMPLE_ROWS, W_SEG), lambda b, c, pt: (b, 0)),
            scratch_shapes=[state, state, state]),
        compiler_params=_cparams(("parallel", "arbitrary"), 48),
        name="sample_sparse_attn",
    )(page_table, qb, bias_past, k_new, v_new, bias_new, h,
      *([cache_k] * n_pages), *([cache_v] * n_pages))


def _merge_kernel(x_ref, pg_ref, ag_ref, g0_ref, g1_ref, g2_ref, g3_ref, gt_ref, wpo_ref, wao_ref, wo_ref,
                  g_ref, b_ref, o_ref, *, alpha):
    def realign(parts):
        return jnp.concatenate(parts, axis=1)[:, GATE_SHIFT:GATE_SHIFT + GATE_W]

    gp = realign([g0_ref[...], g1_ref[...], g2_ref[:, 0:LANES]])
    ga = realign([g2_ref[...], g3_ref[...], gt_ref[...]])
    p = jnp.dot(pg_ref[...], wpo_ref[...], preferred_element_type=F32)
    a = jnp.dot(ag_ref[...], wao_ref[...], preferred_element_type=F32)
    m = jax.nn.sigmoid(gp) * p + jax.nn.sigmoid(ga) * a
    y = jnp.dot(m.astype(BF16), wo_ref[...], preferred_element_type=F32)
    o_ref[...] = _layer_norm_rows(alpha * x_ref[...] + y, g_ref[...], b_ref[...])


def _merge(x, pg, ag, h, wpo, wao, wo, g, b, tm, alpha):
    m, d = x.shape
    const = lambda a: pl.BlockSpec(a.shape, lambda i: (0, 0), pipeline_mode=pl.Buffered(1))
    gate0 = N_MAIN // W_SEG
    gate_blk = lambda k: pl.BlockSpec((tm, W_SEG), lambda i, k=k: (i, gate0 + k))
    gate_tail = pl.BlockSpec((tm, LANES), lambda i: (i, (N_MAIN + 2 * GATE_W) // LANES))
    return pl.pallas_call(
        functools.partial(_merge_kernel, alpha=alpha),
        out_shape=jax.ShapeDtypeStruct((m, d), F32),
        grid=(m // tm,),
        in_specs=[pl.BlockSpec((tm, d), lambda i: (i, 0)),
                  pl.BlockSpec((tm, W_SEG), lambda i: (i, 0)),
                  pl.BlockSpec((tm, W_SEG), lambda i: (i, 0)),
                  gate_blk(0), gate_blk(1), gate_blk(2), gate_blk(3), gate_tail,
                  const(wpo), const(wao), const(wo),
                  pl.BlockSpec((1, d), lambda i: (0, 0)),
                  pl.BlockSpec((1, d), lambda i: (0, 0))],
        out_specs=pl.BlockSpec((tm, d), lambda i: (i, 0)),
        compiler_params=_cparams(("parallel",), 56),
        name="merge_out_ln",
    )(x, pg, ag, h, h, h, h, h, wpo, wao, wo, g.reshape(1, d), b.reshape(1, d))


def kernel(x_prompt, x_sample, cache_k, cache_v, cache_idx_k, state_pool, page_table, ln_emb_g, ln_emb_b, w_in, w_pool_mix, pool_scale, w_pool_out, w_attn_out, w_o, ln_g, ln_b):
    batch, seq, d_model = x_prompt.shape
    dec_batch, dec_seq, _ = x_sample.shape
    depth = w_in.shape[0]
    n_table_pages = page_table.shape[1]
    past = n_table_pages * PAGE_SIZE
    alpha = (2 * depth) ** 0.25
    assert w_in.shape[2] == H_WIDTH and d_model == GATE_W
    assert seq % 512 == 0 and dec_seq <= SAMPLE_ROWS
    assert n_table_pages % SCORE_PAGES_PER_STEP == 0 and n_table_pages % ATTN_PAGES_PER_STEP == 0
    mp = batch * seq
    ms = dec_batch * SAMPLE_ROWS

    xp = _layer_norm(x_prompt.reshape(mp, d_model), ln_emb_g, ln_emb_b, 512)
    xs_in = jnp.pad(x_sample, ((0, 0), (0, SAMPLE_ROWS - dec_seq), (0, 0))).reshape(ms, d_model)
    xs = _layer_norm(xs_in, ln_emb_g, ln_emb_b, ms)

    tab_p = _rope_tables(jnp.arange(seq, dtype=I32))
    tab_s = _rope_tables(jnp.tile(past + jnp.arange(SAMPLE_ROWS, dtype=I32), dec_batch))
    k_top_s = min(TOPK_MAX, (past + dec_seq) // 4)
    real = lambda a, *tail: a.reshape((dec_batch, SAMPLE_ROWS) + tail)[:, :dec_seq]
    cache_k2 = cache_k.reshape(depth, cache_k.shape[1], PAGE_SIZE * N_HEADS, HEAD_DIM)
    cache_v2 = cache_v.reshape(depth, cache_v.shape[1], PAGE_SIZE * N_HEADS, HEAD_DIM)
    cache_ik_t = jnp.swapaxes(cache_idx_k, 2, 3)
    w_in_t = jnp.swapaxes(w_in, 1, 2)

    kp, vp, ikp, plp, kss, vss, iks, pls = [], [], [], [], [], [], [], []
    for l in range(depth):
        wmix = w_pool_mix[l].astype(BF16)
        scale = pool_scale[l].reshape(1, -1)
        wpo, wao, wo = w_pool_out[l].astype(BF16), w_attn_out[l].astype(BF16), w_o[l].astype(BF16)

        h = _project(xp, w_in_t, l, 1024, 512)
        k_f, v_f, qb, kb, vt, iqb, ik_f, ike, iko = _rope(h, tab_p, K_TILE, seq // K_TILE)
        h3 = h.reshape(batch, seq, H_WIDTH)
        pg = _pool(h3, jnp.zeros((batch, HIST_ROWS, W_SEG), F32), wmix, scale, 256, 0)
        ag = _prompt_attention(qb, kb, vt, iqb, ike, iko, h, batch, seq)
        kp.append(k_f.reshape(batch, seq, N_HEADS, HEAD_DIM))
        vp.append(v_f.reshape(batch, seq, N_HEADS, HEAD_DIM))
        ikp.append(ik_f[:, :IDX_DIM].reshape(batch, seq, IDX_DIM))
        plp.append(h3[:, seq - POOL_BUF:, SEG_UP * W_SEG:(SEG_UP + 1) * W_SEG])
        xp = _merge(xp, pg.reshape(mp, W_SEG), ag, h, wpo, wao, wo, ln_g[l], ln_b[l], 256, alpha)

        hs = _project(xs, w_in_t, l, ms, 512)
        k_f, v_f, qb, _, _, iqb, ik_f, _, _ = _rope(hs, tab_s, ms, 1)
        hs3 = hs.reshape(dec_batch, SAMPLE_ROWS, H_WIDTH)
        hist = jnp.pad(state_pool[l], ((0, 0), (HIST_ROWS - POOL_BUF, 0), (0, 0)))
        pg = _pool(hs3, hist, wmix, scale, SAMPLE_ROWS, past)
        iqr = iqb.reshape(dec_batch, SAMPLE_ROWS * N_IDX_HEADS, IDX_DIM)
        wcol = ik_f[:, IDX_DIM:IDX_DIM + N_IDX_HEADS].reshape(dec_batch, SAMPLE_ROWS * N_IDX_HEADS, 1)
        pad_page = lambda a: jnp.pad(a, ((0, 0), (0, PAGE_SIZE - SAMPLE_ROWS)) + ((0, 0),) * (a.ndim - 2))
        ik_new = jnp.swapaxes(pad_page(ik_f[:, :IDX_DIM].reshape(dec_batch, SAMPLE_ROWS, IDX_DIM)), 1, 2)
        sc_past = _sample_scores(page_table, iqr, wcol, cache_ik_t, l,
                                 n_table_pages // SCORE_PAGES_PER_STEP, SCORE_PAGES_PER_STEP)
        sc_new = _sample_scores(page_table, iqr, wcol, ik_new, None, 1, 1)
        bias_past, bias_new = _sample_select(
            sc_past[:, :dec_seq].reshape(dec_batch * dec_seq, past),
            sc_new[:, :dec_seq].reshape(dec_batch * dec_seq, PAGE_SIZE), k_top_s, dec_seq)
        pad_rows = lambda a, w: jnp.pad(a.reshape(dec_batch, dec_seq, w),
                                        ((0, 0), (0, SAMPLE_ROWS - dec_seq), (0, 0)))
        new_page = lambda a: pad_page(a.reshape(dec_batch, SAMPLE_ROWS, N_HEADS, HEAD_DIM)).reshape(
            dec_batch, PAGE_SIZE * N_HEADS, HEAD_DIM)
        ag = _sample_attention(page_table, qb, pad_rows(bias_past, past), new_page(k_f), new_page(v_f),
                               pad_rows(bias_new, PAGE_SIZE), hs, cache_k2, cache_v2, l)
        kss.append(real(k_f, N_HEADS, HEAD_DIM))
        vss.append(real(v_f, N_HEADS, HEAD_DIM))
        iks.append(real(ik_f, LANES)[:, :, :IDX_DIM])
        us = hs3[:, :dec_seq, SEG_UP * W_SEG:(SEG_UP + 1) * W_SEG]
        pls.append(jnp.concatenate([state_pool[l], us], axis=1)[:, -POOL_BUF:])
        xs = _merge(xs, pg.reshape(ms, W_SEG), ag, hs, wpo, wao, wo, ln_g[l], ln_b[l], ms, alpha)

    y_sample = xs.reshape(dec_batch, SAMPLE_ROWS, d_model)[:, :dec_seq]
    return (xp.reshape(batch, seq, d_model), y_sample, jnp.stack(kp), jnp.stack(vp), jnp.stack(ikp),
            jnp.stack(plp), jnp.stack(kss), jnp.stack(vss), jnp.stack(iks), jnp.stack(pls))
```

```python
import functools
import math

import jax
import jax.numpy as jnp
from jax import lax
from jax.experimental import pallas as pl
from jax.experimental.pallas import tpu as pltpu

F32 = jnp.float32
BF16 = jnp.bfloat16
I32 = jnp.int32
I16 = jnp.int16

N_HEADS = 8
HEAD_DIM = 128
N_IDX_HEADS = 16
IDX_DIM = 64
IDX_SCALE = (N_IDX_HEADS * IDX_DIM) ** -0.5
POOL_WINDOWS = (2, 4, 8, 16)
POOL_GROUP = 256
POOL_BUF = 15
TOPK_MAX = 256
ROPE_THETA = 10000.0
LN_EPS = 1e-5
PAGE_SIZE = 128

LANES = 128
HIST_ROWS = 16
SAMPLE_ROWS = 16
SMALL_W = 256
Q_TILE = 256
K_TILE = 256
SCORE_ROWS = 128
ACC_ROWS = 64
SMALL_ACC_ROWS = 8
SCORE_PAGES_PER_STEP = 16
ATTN_PAGES_PER_STEP = 8

W_SEG = 1024
SEG_UP, SEG_ZP, SEG_Q, SEG_K, SEG_V, SEG_ZA, SEG_IQ = 0, 1, 2, 3, 4, 5, 6
N_MAIN = 7 * W_SEG
SEG_SMALL = N_MAIN // SMALL_W
GATE_SHIFT = IDX_DIM + N_IDX_HEADS
GATE_W = 2048
H_WIDTH = N_MAIN + GATE_SHIFT + 2 * GATE_W

INT_MIN = -2 ** 31
HALF_BITS = 16
HALF_MASK = (1 << HALF_BITS) - 1
HALF_OFFSET = 1 << (HALF_BITS - 1)
NT_DIMS = (((1,), (1,)), ((), ()))
MASK_NEG = float("-inf")
M_INIT = -1e30
EXP2_SCALE = HEAD_DIM ** -0.5 * math.log2(math.e)


def _cparams(sem, vmem_mb):
    return pltpu.CompilerParams(dimension_semantics=sem, vmem_limit_bytes=vmem_mb * 1024 * 1024)


def _silu(z):
    return z * jax.nn.sigmoid(z)


def _layer_norm_rows(x, g, b):
    mu = jnp.mean(x, axis=-1, keepdims=True)
    xc = x - mu
    var = jnp.mean(xc * xc, axis=-1, keepdims=True)
    return xc * lax.rsqrt(var + LN_EPS) * g + b


def _ln_kernel(x_ref, g_ref, b_ref, o_ref):
    o_ref[...] = _layer_norm_rows(x_ref[...], g_ref[...], b_ref[...])


def _layer_norm(x, g, b, tm):
    m, d = x.shape
    return pl.pallas_call(
        _ln_kernel,
        out_shape=jax.ShapeDtypeStruct((m, d), F32),
        grid=(m // tm,),
        in_specs=[pl.BlockSpec((tm, d), lambda i: (i, 0)),
                  pl.BlockSpec((1, d), lambda i: (0, 0)),
                  pl.BlockSpec((1, d), lambda i: (0, 0))],
        out_specs=pl.BlockSpec((tm, d), lambda i: (i, 0)),
        compiler_params=_cparams(("parallel",), 32),
        name="ln_embed",
    )(x, g.reshape(1, d), b.reshape(1, d))


def _proj_kernel(x_ref, w_ref, wt_ref, o_ref, ot_ref, xb_ref):
    @pl.when(pl.program_id(1) == 0)
    def _():
        xb = x_ref[...].astype(BF16)
        xb_ref[...] = xb
        ot_ref[...] = lax.dot_general(xb, wt_ref[...].astype(BF16), NT_DIMS,
                                      preferred_element_type=F32)

    o_ref[...] = lax.dot_general(xb_ref[...], w_ref[...].astype(BF16), NT_DIMS,
                                 preferred_element_type=F32)


def _project(x, w_t, layer, tm, tn):
    m, d = x.shape
    n = w_t.shape[1]
    n_main = (n // tn) * tn
    assert n - n_main <= LANES and n_main % LANES == 0
    return pl.pallas_call(
        _proj_kernel,
        out_shape=(jax.ShapeDtypeStruct((m, n_main), F32), jax.ShapeDtypeStruct((m, LANES), F32)),
        grid=(m // tm, n_main // tn),
        in_specs=[pl.BlockSpec((tm, d), lambda i, j: (i, 0), pipeline_mode=pl.Buffered(1)),
                  pl.BlockSpec((None, tn, d), lambda i, j: (layer, j, 0)),
                  pl.BlockSpec((None, LANES, d), lambda i, j: (layer, n_main // LANES, 0))],
        out_specs=(pl.BlockSpec((tm, tn), lambda i, j: (i, j)),
                   pl.BlockSpec((tm, LANES), lambda i, j: (i, 0))),
        scratch_shapes=[pltpu.VMEM((tm, d), BF16)],
        compiler_params=_cparams(("parallel", "arbitrary"), 48),
        name="in_proj",
    )(x, w_t, w_t)


def _rope_kernel(q_ref, k_ref, v_ref, iq_ref, sm_ref, tab_ref,
                 ko_ref, vo_ref, qb_ref, kb_ref, vt_ref, iqb_ref, iks_ref, ike_ref, iko_ref):
    cos_h, sin_h = tab_ref[0], tab_ref[1]
    cos_i, sin_i = tab_ref[2], tab_ref[3]
    cos_s, sin_s = tab_ref[4], tab_ref[5]
    lane = lax.broadcasted_iota(I32, cos_h.shape, 1)
    low_half = (lane % IDX_DIM) < (IDX_DIM // 2)

    def rope_idx(x, cos, sin):
        partner = jnp.where(low_half, pltpu.roll(x, LANES - IDX_DIM // 2, 1),
                            pltpu.roll(x, IDX_DIM // 2, 1))
        return x * cos + partner * sin

    for h in range(N_HEADS):
        sl = slice(h * HEAD_DIM, (h + 1) * HEAD_DIM)
        x = q_ref[:, sl]
        qb_ref[:, sl] = (x * cos_h + pltpu.roll(x, HEAD_DIM // 2, 1) * sin_h).astype(BF16)
        x = k_ref[:, sl]
        kr = x * cos_h + pltpu.roll(x, HEAD_DIM // 2, 1) * sin_h
        ko_ref[:, sl] = kr
        kb_ref[:, sl] = kr.astype(BF16)
        iqb_ref[:, sl] = rope_idx(iq_ref[:, sl], cos_i, sin_i).astype(BF16)
        v = v_ref[:, sl]
        vo_ref[:, sl] = v
        vt_ref[0, sl, :] = v.T.astype(BF16)
    xs = rope_idx(sm_ref[:, 0:LANES], cos_s, sin_s)
    iks_ref[...] = xs
    ik_only = jnp.where(lane < IDX_DIM, xs, 0.0)
    ike_ref[...] = ik_only.astype(BF16)
    iko_ref[...] = pltpu.roll(ik_only, IDX_DIM, 1).astype(BF16)


def _rope(h, tab, tm, tab_blocks):
    m = h.shape[0]
    seg = lambda s: pl.BlockSpec((tm, W_SEG), lambda i, s=s: (i, s))
    row = lambda w: pl.BlockSpec((tm, w), lambda i: (i, 0))
    f32o = lambda w: jax.ShapeDtypeStruct((m, w), F32)
    b16o = lambda w: jax.ShapeDtypeStruct((m, w), BF16)
    return pl.pallas_call(
        _rope_kernel,
        out_shape=(f32o(W_SEG), f32o(W_SEG), b16o(W_SEG), b16o(W_SEG),
                   jax.ShapeDtypeStruct((m // tm, W_SEG, tm), BF16), b16o(W_SEG),
                   f32o(LANES), b16o(LANES), b16o(LANES)),
        grid=(m // tm,),
        in_specs=[seg(SEG_Q), seg(SEG_K), seg(SEG_V), seg(SEG_IQ),
                  pl.BlockSpec((tm, SMALL_W), lambda i: (i, SEG_SMALL)),
                  pl.BlockSpec((6, tm, LANES), lambda i: (0, i % tab_blocks, 0))],
        out_specs=(row(W_SEG), row(W_SEG), row(W_SEG), row(W_SEG),
                   pl.BlockSpec((1, W_SEG, tm), lambda i: (i, 0, 0)), row(W_SEG),
                   row(LANES), row(LANES), row(LANES)),
        compiler_params=_cparams(("parallel",), 48),
        name="rope_split",
    )(h, h, h, h, h, tab)


def _rope_tables(pos):
    def cs(half):
        inv = ROPE_THETA ** (-jnp.arange(half, dtype=F32) / half)
        ang = pos.astype(F32)[:, None] * inv[None, :]
        return jnp.cos(ang), jnp.sin(ang)
    ch, sh = cs(HEAD_DIM // 2)
    ci, si = cs(IDX_DIM // 2)
    n = pos.shape[0]
    one = jnp.ones((n, LANES - IDX_DIM), F32)
    return jnp.stack([
        jnp.concatenate([ch, ch], axis=1), jnp.concatenate([-sh, sh], axis=1),
        jnp.concatenate([ci, ci, ci, ci], axis=1), jnp.concatenate([-si, si, -si, si], axis=1),
        jnp.concatenate([ci, ci, one], axis=1), jnp.concatenate([-si, si, 0.0 * one], axis=1)])


def _pool_kernel(up_ref, zp_ref, hist_ref, wmix_ref, scale_ref, o_ref, ext_ref, *, tm, pos0):
    t = pl.program_id(1)

    @pl.when(t == 0)
    def _():
        ext_ref[0:HIST_ROWS, :] = hist_ref[0]

    @pl.when(t > 0)
    def _():
        ext_ref[0:HIST_ROWS, :] = ext_ref[tm:tm + HIST_ROWS, :]

    u = up_ref[0]
    ext_ref[HIST_ROWS:HIST_ROWS + tm, :] = u
    pos = pos0 + t * tm + lax.broadcasted_iota(I32, (tm, 1), 0)
    for g, w in enumerate(POOL_WINDOWS):
        sl = slice(g * POOL_GROUP, (g + 1) * POOL_GROUP)
        acc = u[:, sl]
        for j in range(1, w):
            acc = acc + ext_ref[HIST_ROWS - j:HIST_ROWS - j + tm, sl]
        cnt = jnp.minimum(pos + 1, w).astype(F32)
        d = acc / cnt - u[:, sl]
        mixed = jnp.dot(d.astype(BF16), wmix_ref[g], preferred_element_type=F32)
        o_ref[0, :, sl] = (mixed * scale_ref[:, sl] * _silu(zp_ref[0, :, sl])).astype(BF16)


def _pool(h3, hist, wmix, scale, tm, pos0):
    b, t, _ = h3.shape
    width = scale.shape[1]
    return pl.pallas_call(
        functools.partial(_pool_kernel, tm=tm, pos0=pos0),
        out_shape=jax.ShapeDtypeStruct((b, t, width), BF16),
        grid=(b, t // tm),
        in_specs=[pl.BlockSpec((1, tm, W_SEG), lambda i, j: (i, j, SEG_UP)),
                  pl.BlockSpec((1, tm, W_SEG), lambda i, j: (i, j, SEG_ZP)),
                  pl.BlockSpec((1, HIST_ROWS, width), lambda i, j: (i, 0, 0)),
                  pl.BlockSpec(wmix.shape, lambda i, j: (0, 0, 0)),
                  pl.BlockSpec((1, width), lambda i, j: (0, 0))],
        out_specs=pl.BlockSpec((1, tm, width), lambda i, j: (i, j, 0)),
        scratch_shapes=[pltpu.VMEM((HIST_ROWS + tm, width), F32)],
        compiler_params=_cparams(("parallel", "arbitrary"), 32),
        name="pool_mixer",
    )(h3, h3, hist, wmix, scale)


def _float_key(x):
    bits = pltpu.bitcast(x, I32)
    return bits ^ ((bits >> 31) & jnp.int32(0x7FFFFFFF))


def _radix_threshold(count_ge, shape, k_top, bits):
    sign = jnp.int32(-(1 << (bits - 1)))

    def bit_body(i, t_u):
        c_u = t_u | (jnp.int32(1) << (bits - 1 - i))
        return jnp.where(count_ge(c_u + sign) >= k_top, c_u, t_u)

    return lax.fori_loop(0, bits, bit_body, jnp.zeros(shape, I32)) + sign


def _resolve_ties(thr, count_gt_eq, count_eq_below, shape, k_top, idx_bits, j_ref):
    n_gt, n_eq = count_gt_eq(thr)
    need = k_top - n_gt
    j_ref[...] = jnp.full(shape, (1 << idx_bits) - 1, I32)

    @pl.when(jnp.max(n_eq - need) > 0.0)
    def _():
        def idx_body(i, j):
            c = j | (jnp.int32(1) << (idx_bits - 1 - i))
            return jnp.where(count_eq_below(thr, c) < need, c, j)
        j_ref[...] = lax.fori_loop(0, idx_bits, idx_body, jnp.zeros(shape, I32))


def _count_lanes(mask):
    x = jnp.where(mask, 1.0, 0.0)
    w = x.shape[1]
    while w > 8 * LANES and w % (2 * LANES) == 0:
        w //= 2
        x = x[:, :w] + x[:, w:]
    return jnp.sum(x, axis=1, keepdims=True)


def _prompt_attn_kernel(q_ref, k_ref, vt_ref, iq_ref, ike_ref, iko_ref, sm_ref, za_ref, o_ref,
                        key_ref, hi_ref, lo_ref, bias_ref, s_ref, o_acc_ref, j_ref, *, k_top, idx_bits):
    i = pl.program_id(1)
    n_chunks = i + 1
    qpos = i * Q_TILE + lax.broadcasted_iota(I32, (1, Q_TILE), 1)
    sm_t = sm_ref[:, 0:LANES].T
    w_t = sm_t[IDX_DIM:IDX_DIM + N_IDX_HEADS, :] * IDX_SCALE

    def score_chunk(c, carry):
        for half in range(K_TILE // SCORE_ROWS):
            r0 = pl.multiple_of(c * K_TILE + half * SCORE_ROWS, SCORE_ROWS)
            ike = ike_ref[pl.ds(r0, SCORE_ROWS), :]
            iko = iko_ref[pl.ds(r0, SCORE_ROWS), :]
            acc = jnp.zeros((SCORE_ROWS, Q_TILE), F32)
            for p in range(N_IDX_HEADS // 2):
                rhs = iq_ref[:, p * LANES:(p + 1) * LANES]
                lg_e = lax.dot_general(ike, rhs, NT_DIMS, preferred_element_type=F32)
                lg_o = lax.dot_general(iko, rhs, NT_DIMS, preferred_element_type=F32)
                acc = acc + jnp.maximum(lg_e, 0.0) * w_t[2 * p:2 * p + 1, :]
                acc = acc + jnp.maximum(lg_o, 0.0) * w_t[2 * p + 1:2 * p + 2, :]
            kidx = r0 + lax.broadcasted_iota(I32, (SCORE_ROWS, 1), 0)
            key = jnp.where(kidx <= qpos, _float_key(acc), jnp.int32(INT_MIN))
            key_ref[pl.ds(r0, SCORE_ROWS), :] = key
            hi_ref[pl.ds(r0, SCORE_ROWS), :] = (key >> HALF_BITS).astype(I16)
            lo_ref[pl.ds(r0, SCORE_ROWS), :] = ((key & HALF_MASK) - HALF_OFFSET).astype(I16)
        return carry

    lax.fori_loop(0, n_chunks, score_chunk, 0)

    def fold(x):
        parts = x.reshape(K_TILE // ACC_ROWS, ACC_ROWS, Q_TILE)
        return functools.reduce(lambda a, b: a + b, [parts[t] for t in range(K_TILE // ACC_ROWS)])

    def count_where(src_ref, pred, one, zero):
        def body(c, acc):
            r0 = pl.multiple_of(c * K_TILE, K_TILE)
            kidx = r0 + lax.broadcasted_iota(I32, (K_TILE, 1), 0)
            return acc + fold(jnp.where(pred(src_ref[pl.ds(r0, K_TILE), :], kidx), one, zero))
        acc = lax.fori_loop(0, n_chunks, body, jnp.zeros((ACC_ROWS, Q_TILE), one.dtype))
        return jnp.sum(acc.astype(I32).astype(F32), axis=0, keepdims=True)

    def count16(src_ref, pred):
        return count_where(src_ref, pred, jnp.int16(1), jnp.int16(0))

    def count32(pred):
        return count_where(key_ref, pred, jnp.float32(1.0), jnp.float32(0.0))

    t_hi = _radix_threshold(lambda c: count16(hi_ref, lambda v, kidx: v >= c.astype(I16)),
                            (1, Q_TILE), float(k_top), HALF_BITS)
    t_hi16 = t_hi.astype(I16)
    n_above = count16(hi_ref, lambda v, kidx: v > t_hi16)

    def mask_low(c, carry):
        r0 = pl.multiple_of(c * K_TILE, K_TILE)
        lo_ref[pl.ds(r0, K_TILE), :] = jnp.where(hi_ref[pl.ds(r0, K_TILE), :] == t_hi16,
                                                 lo_ref[pl.ds(r0, K_TILE), :], jnp.int16(-HALF_OFFSET))
        return carry

    lax.fori_loop(0, n_chunks, mask_low, 0)
    t_lo = _radix_threshold(lambda c: n_above + count16(lo_ref, lambda v, kidx: v >= c.astype(I16)),
                            (1, Q_TILE), float(k_top), HALF_BITS)
    thr = t_hi * (HALF_MASK + 1) + (t_lo + HALF_OFFSET)

    def count_gt_eq(thr):
        return (count32(lambda keys, kidx: keys > thr),
                count32(lambda keys, kidx: (keys == thr) & (kidx <= qpos)))

    def count_eq_below(thr, c):
        return count32(lambda keys, kidx: (keys == thr) & (kidx <= qpos) & (kidx < c))

    _resolve_ties(thr, count_gt_eq, count_eq_below, (1, Q_TILE), float(k_top), idx_bits, j_ref)
    j_sel = j_ref[...]

    def bias_chunk(c, carry):
        r0 = pl.multiple_of(c * K_TILE, K_TILE)
        kidx = r0 + lax.broadcasted_iota(I32, (K_TILE, 1), 0)
        keys = key_ref[pl.ds(r0, K_TILE), :]
        sel = ((keys > thr) | ((keys == thr) & (kidx <= j_sel))) & (kidx <= qpos)
        bias_ref[pl.ds(r0, K_TILE), :] = jnp.where(sel, 0.0, MASK_NEG)
        return carry

    lax.fori_loop(0, n_chunks, bias_chunk, 0)

    heads = [slice(h * HEAD_DIM, (h + 1) * HEAD_DIM) for h in range(N_HEADS)]

    def fold_small(x, op):
        return op(x.reshape(K_TILE // SMALL_ACC_ROWS, SMALL_ACC_ROWS, Q_TILE), axis=0)

    def raw_scores(c, m_accs):
        r0 = pl.multiple_of(c * K_TILE, K_TILE)
        out = []
        for h, sl in enumerate(heads):
            s = lax.dot_general(k_ref[pl.ds(r0, K_TILE), sl], q_ref[:, sl], NT_DIMS,
                                preferred_element_type=F32)
            s = s + bias_ref[pl.ds(r0, K_TILE), :]
            s_ref[h, pl.ds(r0, K_TILE), :] = s
            out.append(jnp.maximum(m_accs[h], fold_small(s, jnp.max)))
        return tuple(out)

    m_accs = lax.fori_loop(
        0, n_chunks, raw_scores,
        tuple(jnp.full((SMALL_ACC_ROWS, Q_TILE), MASK_NEG, F32) for _ in heads))
    m_heads = [jnp.max(a, axis=0, keepdims=True) for a in m_accs]
    o_acc_ref[...] = jnp.zeros(o_acc_ref.shape, F32)

    def weighted_values(c, l_accs):
        r0 = pl.multiple_of(c * K_TILE, K_TILE)
        out = []
        for h, sl in enumerate(heads):
            p = jnp.exp2((s_ref[h, pl.ds(r0, K_TILE), :] - m_heads[h]) * EXP2_SCALE)
            o_acc_ref[h] += jnp.dot(vt_ref[c, sl, :], p.astype(BF16), preferred_element_type=F32)
            out.append(l_accs[h] + fold_small(p, jnp.sum))
        return tuple(out)

    l_accs = lax.fori_loop(
        0, n_chunks, weighted_values,
        tuple(jnp.zeros((SMALL_ACC_ROWS, Q_TILE), F32) for _ in heads))
    for h, sl in enumerate(heads):
        o_t = o_acc_ref[h] / jnp.sum(l_accs[h], axis=0, keepdims=True)
        o_ref[:, sl] = (o_t.T * _silu(za_ref[:, sl])).astype(BF16)


def _prompt_attention(qb, kb, vt, iqb, ike, iko, h, batch, seq):
    m = qb.shape[0]
    nq = seq // Q_TILE
    k_top = min(TOPK_MAX, seq // 4)
    qrow = lambda w: pl.BlockSpec((Q_TILE, w), lambda b, i: (b * nq + i, 0))
    kv = lambda w: pl.BlockSpec((seq, w), lambda b, i: (b, 0), pipeline_mode=pl.Buffered(1))
    return pl.pallas_call(
        functools.partial(_prompt_attn_kernel, k_top=k_top, idx_bits=(seq - 1).bit_length()),
        out_shape=jax.ShapeDtypeStruct((m, W_SEG), BF16),
        grid=(batch, nq),
        in_specs=[qrow(W_SEG), kv(W_SEG),
                  pl.BlockSpec((seq // K_TILE, W_SEG, K_TILE), lambda b, i: (b, 0, 0),
                               pipeline_mode=pl.Buffered(1)),
                  qrow(W_SEG), kv(LANES), kv(LANES),
                  pl.BlockSpec((Q_TILE, SMALL_W), lambda b, i: (b * nq + i, SEG_SMALL)),
                  pl.BlockSpec((Q_TILE, W_SEG), lambda b, i: (b * nq + i, SEG_ZA))],
        out_specs=qrow(W_SEG),
        scratch_shapes=[pltpu.VMEM((seq, Q_TILE), I32), pltpu.VMEM((seq, Q_TILE), I16),
                        pltpu.VMEM((seq, Q_TILE), I16), pltpu.VMEM((seq, Q_TILE), F32),
                        pltpu.VMEM((N_HEADS, seq, Q_TILE), F32),
                        pltpu.VMEM((N_HEADS, HEAD_DIM, Q_TILE), F32),
                        pltpu.VMEM((1, Q_TILE), I32)],
        compiler_params=_cparams(("parallel", "arbitrary"), 56),
        name="prompt_sparse_attn",
    )(qb, kb, vt, iqb, ike, iko, h, h)


def _sample_scores_kernel(pt_ref, iqr_ref, wcol_ref, *rest, n_pages):
    del pt_ref
    page_refs, o_ref = rest[:n_pages], rest[n_pages]
    iqr = iqr_ref[0]
    w = wcol_ref[0] * IDX_SCALE
    for j in range(n_pages):
        ikp_t = page_refs[j][...].astype(BF16)
        lg = jnp.dot(iqr, ikp_t, preferred_element_type=F32)
        t = jnp.maximum(lg, 0.0) * w
        sc = t.reshape(SAMPLE_ROWS, N_IDX_HEADS, PAGE_SIZE).sum(axis=1)
        o_ref[0, :, j * PAGE_SIZE:(j + 1) * PAGE_SIZE] = 0.0 + sc


def _sample_scores(page_table, iqr, wcol, pages, layer, n_chunks, n_pages):
    db = iqr.shape[0]
    if layer is not None:
        page_spec = lambda j: pl.BlockSpec(
            (None, None, IDX_DIM, PAGE_SIZE),
            lambda b, c, pt, j=j: (layer, pt[b, c * n_pages + j], 0, 0))
    else:
        page_spec = lambda j: pl.BlockSpec((None, IDX_DIM, PAGE_SIZE), lambda b, c, pt: (b, 0, 0))
    rows = SAMPLE_ROWS * N_IDX_HEADS
    return pl.pallas_call(
        functools.partial(_sample_scores_kernel, n_pages=n_pages),
        out_shape=jax.ShapeDtypeStruct((db, SAMPLE_ROWS, n_chunks * n_pages * PAGE_SIZE), F32),
        grid_spec=pltpu.PrefetchScalarGridSpec(
            num_scalar_prefetch=1,
            grid=(db, n_chunks),
            in_specs=[pl.BlockSpec((1, rows, IDX_DIM), lambda b, c, pt: (b, 0, 0)),
                      pl.BlockSpec((1, rows, 1), lambda b, c, pt: (b, 0, 0))]
                     + [page_spec(j) for j in range(n_pages)],
            out_specs=pl.BlockSpec((1, SAMPLE_ROWS, n_pages * PAGE_SIZE), lambda b, c, pt: (b, 0, c))),
        compiler_params=_cparams(("parallel", "arbitrary"), 32),
        name="sample_idx_scores",
    )(page_table, iqr, wcol, *([pages] * n_pages))


def _sample_select_kernel(sp_ref, sn_ref, bp_ref, bn_ref, keyp_ref, keyn_ref, j_ref, *,
                          k_top, past, rows_per_seq):
    rows = sp_ref.shape[0]
    n = lax.broadcasted_iota(I32, (rows, 1), 0) % rows_per_seq
    colp = lax.broadcasted_iota(I32, (1, past), 1)
    coln = lax.broadcasted_iota(I32, (1, PAGE_SIZE), 1)
    validn = coln <= n
    keyp_ref[...] = _float_key(sp_ref[...])
    keyn_ref[...] = jnp.where(validn, _float_key(sn_ref[...]), jnp.int32(INT_MIN))

    def count_ge(c):
        return _count_lanes(keyp_ref[...] >= c) + _count_lanes(keyn_ref[...] >= c)

    def count_gt_eq(thr):
        kp, kn = keyp_ref[...], keyn_ref[...]
        return (_count_lanes(kp > thr) + _count_lanes(kn > thr),
                _count_lanes(kp == thr) + _count_lanes((kn == thr) & validn))

    def count_eq_below(thr, c):
        return (_count_lanes((keyp_ref[...] == thr) & (colp < c))
                + _count_lanes((keyn_ref[...] == thr) & validn & (coln + past < c)))

    thr = _radix_threshold(count_ge, (rows, 1), float(k_top), 32)
    _resolve_ties(thr, count_gt_eq, count_eq_below, (rows, 1), float(k_top),
                  (past + PAGE_SIZE - 1).bit_length(), j_ref)
    j = j_ref[...]
    kp, kn = keyp_ref[...], keyn_ref[...]
    bp_ref[...] = jnp.where((kp > thr) | ((kp == thr) & (colp <= j)), 0.0, MASK_NEG)
    sel_n = ((kn > thr) | ((kn == thr) & (coln + past <= j))) & validn
    bn_ref[...] = jnp.where(sel_n, 0.0, MASK_NEG)


def _sample_select(sc_past, sc_new, k_top, rows_per_seq):
    rows, past = sc_past.shape
    blk = lambda w: pl.BlockSpec((rows, w), lambda i: (0, 0))
    return pl.pallas_call(
        functools.partial(_sample_select_kernel, k_top=k_top, past=past, rows_per_seq=rows_per_seq),
        out_shape=(jax.ShapeDtypeStruct(sc_past.shape, F32), jax.ShapeDtypeStruct(sc_new.shape, F32)),
        grid=(1,),
        in_specs=[blk(past), blk(PAGE_SIZE)],
        out_specs=(blk(past), blk(PAGE_SIZE)),
        scratch_shapes=[pltpu.VMEM((rows, past), I32), pltpu.VMEM((rows, PAGE_SIZE), I32),
                        pltpu.VMEM((rows, 1), I32)],
        compiler_params=_cparams(("arbitrary",), 48),
        name="sample_topk_select",
    )(sc_past, sc_new)


def _sample_attn_kernel(pt_ref, q_ref, bias_ref, knew_ref, vnew_ref, bnew_ref, za_ref, *rest,
                        n_pages, n_chunks):
    del pt_ref
    k_refs, v_refs = rest[:n_pages], rest[n_pages:2 * n_pages]
    o_ref, m_ref, l_ref, acc_ref = rest[2 * n_pages:]
    c = pl.program_id(1)

    @pl.when(c == 0)
    def _():
        m_ref[...] = jnp.full(m_ref.shape, M_INIT, F32)
        l_ref[...] = jnp.zeros(l_ref.shape, F32)
        acc_ref[...] = jnp.zeros(acc_ref.shape, F32)

    def head_rows(page_ref, h):
        return page_ref[pl.ds(h, PAGE_SIZE, stride=N_HEADS), :].astype(BF16)

    def process(k_pages, v_pages, bias):
        s_heads = []
        for h in range(N_HEADS):
            qh = q_ref[:, h * HEAD_DIM:(h + 1) * HEAD_DIM]
            s_heads.append(jnp.concatenate(
                [lax.dot_general(qh, head_rows(kp, h), NT_DIMS, preferred_element_type=F32)
                 for kp in k_pages], axis=1) + bias)
        s = jnp.concatenate(s_heads, axis=0)
        m_old = m_ref[...]
        m_new = jnp.maximum(m_old, jnp.max(s, axis=-1, keepdims=True))
        alpha = jnp.exp2((m_old - m_new) * EXP2_SCALE)
        p = jnp.exp2((s - m_new[:, 0:1]) * EXP2_SCALE)
        l_ref[...] = alpha * l_ref[...] + jnp.sum(p, axis=-1, keepdims=True)
        pv_heads = []
        for h in range(N_HEADS):
            pv = jnp.zeros((SAMPLE_ROWS, HEAD_DIM), F32)
            for j, vp in enumerate(v_pages):
                ph = p[h * SAMPLE_ROWS:(h + 1) * SAMPLE_ROWS, j * PAGE_SIZE:(j + 1) * PAGE_SIZE]
                pv = pv + jnp.dot(ph.astype(BF16), head_rows(vp, h), preferred_element_type=F32)
            pv_heads.append(pv)
        acc_ref[...] = alpha * acc_ref[...] + jnp.concatenate(pv_heads, axis=0)
        m_ref[...] = m_new

    @pl.when(c < n_chunks)
    def _():
        process(k_refs, v_refs, bias_ref[0])

    @pl.when(c == n_chunks)
    def _():
        process([knew_ref], [vnew_ref], bnew_ref[0])
        o = acc_ref[...] / l_ref[...]
        for h in range(N_HEADS):
            sl = slice(h * HEAD_DIM, (h + 1) * HEAD_DIM)
            o_ref[:, sl] = (o[h * SAMPLE_ROWS:(h + 1) * SAMPLE_ROWS] * _silu(za_ref[:, sl])).astype(BF16)


def _sample_attention(page_table, qb, bias_past, k_new, v_new, bias_new, h, cache_k, cache_v, layer):
    db, n_table_pages = page_table.shape
    n_pages = ATTN_PAGES_PER_STEP
    n_chunks = n_table_pages // n_pages
    last = n_table_pages - 1
    page_rows = PAGE_SIZE * N_HEADS
    page_spec = lambda j: pl.BlockSpec(
        (None, None, page_rows, HEAD_DIM),
        lambda b, c, pt, j=j: (layer, pt[b, jnp.minimum(c * n_pages + j, last)], 0, 0))
    new_page = pl.BlockSpec((None, page_rows, HEAD_DIM), lambda b, c, pt: (b, 0, 0))
    state = pltpu.VMEM((N_HEADS * SAMPLE_ROWS, LANES), F32)
    return pl.pallas_call(
        functools.partial(_sample_attn_kernel, n_pages=n_pages, n_chunks=n_chunks),
        out_shape=jax.ShapeDtypeStruct((db * SAMPLE_ROWS, W_SEG), BF16),
        grid_spec=pltpu.PrefetchScalarGridSpec(
            num_scalar_prefetch=1,
            grid=(db, n_chunks + 1),
            in_specs=[pl.BlockSpec((SAMPLE_ROWS, W_SEG), lambda b, c, pt: (b, 0)),
                      pl.BlockSpec((1, SAMPLE_ROWS, n_pages * PAGE_SIZE),
                                   lambda b, c, pt: (b, 0, jnp.minimum(c, n_chunks - 1))),
                      new_page, new_page,
                      pl.BlockSpec((1, SAMPLE_ROWS, PAGE_SIZE), lambda b, c, pt: (b, 0, 0)),
                      pl.BlockSpec((SAMPLE_ROWS, W_SEG), lambda b, c, pt: (b, SEG_ZA))]
                     + [page_spec(j) for j in range(n_pages)] * 2,
            out_specs=pl.BlockSpec((SAMPLE_ROWS, W_SEG), lambda b, c, pt: (b, 0)),
            scratch_shapes=[state, state, state]),
        compiler_params=_cparams(("parallel", "arbitrary"), 48),
        name="sample_sparse_attn",
    )(page_table, qb, bias_past, k_new, v_new, bias_new, h,
      *([cache_k] * n_pages), *([cache_v] * n_pages))


def _merge_kernel(x_ref, pg_ref, ag_ref, g0_ref, g1_ref, g2_ref, g3_ref, gt_ref, wpo_ref, wao_ref, wo_ref,
                  g_ref, b_ref, o_ref, *, alpha):
    def realign(parts):
        return jnp.concatenate(parts, axis=1)[:, GATE_SHIFT:GATE_SHIFT + GATE_W]

    gp = realign([g0_ref[...], g1_ref[...], g2_ref[:, 0:LANES]])
    ga = realign([g2_ref[...], g3_ref[...], gt_ref[...]])
    p = jnp.dot(pg_ref[...], wpo_ref[...], preferred_element_type=F32)
    a = jnp.dot(ag_ref[...], wao_ref[...], preferred_element_type=F32)
    m = jax.nn.sigmoid(gp) * p + jax.nn.sigmoid(ga) * a
    y = jnp.dot(m.astype(BF16), wo_ref[...], preferred_element_type=F32)
    o_ref[...] = _layer_norm_rows(alpha * x_ref[...] + y, g_ref[...], b_ref[...])


def _merge(x, pg, ag, h, h_tail, wpo, wao, wo, g, b, tm, alpha):
    m, d = x.shape
    const = lambda a: pl.BlockSpec(a.shape, lambda i: (0, 0), pipeline_mode=pl.Buffered(1))
    gate0 = N_MAIN // W_SEG
    gate_blk = lambda k: pl.BlockSpec((tm, W_SEG), lambda i, k=k: (i, gate0 + k))
    gate_tail = pl.BlockSpec((tm, LANES), lambda i: (i, 0))
    return pl.pallas_call(
        functools.partial(_merge_kernel, alpha=alpha),
        out_shape=jax.ShapeDtypeStruct((m, d), F32),
        grid=(m // tm,),
        in_specs=[pl.BlockSpec((tm, d), lambda i: (i, 0)),
                  pl.BlockSpec((tm, W_SEG), lambda i: (i, 0)),
                  pl.BlockSpec((tm, W_SEG), lambda i: (i, 0)),
                  gate_blk(0), gate_blk(1), gate_blk(2), gate_blk(3), gate_tail,
                  const(wpo), const(wao), const(wo),
                  pl.BlockSpec((1, d), lambda i: (0, 0)),
                  pl.BlockSpec((1, d), lambda i: (0, 0))],
        out_specs=pl.BlockSpec((tm, d), lambda i: (i, 0)),
        compiler_params=_cparams(("parallel",), 56),
        name="merge_out_ln",
    )(x, pg, ag, h, h, h, h, h_tail, wpo, wao, wo, g.reshape(1, d), b.reshape(1, d))


def kernel(x_prompt, x_sample, cache_k, cache_v, cache_idx_k, state_pool, page_table, ln_emb_g, ln_emb_b, w_in, w_pool_mix, pool_scale, w_pool_out, w_attn_out, w_o, ln_g, ln_b):
    batch, seq, d_model = x_prompt.shape
    dec_batch, dec_seq, _ = x_sample.shape
    depth = w_in.shape[0]
    n_table_pages = page_table.shape[1]
    past = n_table_pages * PAGE_SIZE
    alpha = (2 * depth) ** 0.25
    assert w_in.shape[2] == H_WIDTH and d_model == GATE_W
    assert seq % 512 == 0 and dec_seq <= SAMPLE_ROWS
    assert n_table_pages % SCORE_PAGES_PER_STEP == 0 and n_table_pages % ATTN_PAGES_PER_STEP == 0
    mp = batch * seq
    ms = dec_batch * SAMPLE_ROWS

    xp = _layer_norm(x_prompt.reshape(mp, d_model), ln_emb_g, ln_emb_b, 512)
    xs_in = jnp.pad(x_sample, ((0, 0), (0, SAMPLE_ROWS - dec_seq), (0, 0))).reshape(ms, d_model)
    xs = _layer_norm(xs_in, ln_emb_g, ln_emb_b, ms)

    tab_p = _rope_tables(jnp.arange(seq, dtype=I32))
    tab_s = _rope_tables(jnp.tile(past + jnp.arange(SAMPLE_ROWS, dtype=I32), dec_batch))
    k_top_s = min(TOPK_MAX, (past + dec_seq) // 4)
    real = lambda a, *tail: a.reshape((dec_batch, SAMPLE_ROWS) + tail)[:, :dec_seq]
    cache_k2 = cache_k.reshape(depth, cache_k.shape[1], PAGE_SIZE * N_HEADS, HEAD_DIM)
    cache_v2 = cache_v.reshape(depth, cache_v.shape[1], PAGE_SIZE * N_HEADS, HEAD_DIM)
    cache_ik_t = jnp.swapaxes(cache_idx_k, 2, 3)
    w_in_t = jnp.swapaxes(w_in, 1, 2)

    kp, vp, ikp, plp, kss, vss, iks, pls = [], [], [], [], [], [], [], []
    for l in range(depth):
        wmix = w_pool_mix[l].astype(BF16)
        scale = pool_scale[l].reshape(1, -1)
        wpo, wao, wo = w_pool_out[l].astype(BF16), w_attn_out[l].astype(BF16), w_o[l].astype(BF16)

        h, h_tail = _project(xp, w_in_t, l, 1024, 1024)
        k_f, v_f, qb, kb, vt, iqb, ik_f, ike, iko = _rope(h, tab_p, K_TILE, seq // K_TILE)
        h3 = h.reshape(batch, seq, h.shape[1])
        pg = _pool(h3, jnp.zeros((batch, HIST_ROWS, W_SEG), F32), wmix, scale, 256, 0)
        ag = _prompt_attention(qb, kb, vt, iqb, ike, iko, h, batch, seq)
        kp.append(k_f.reshape(batch, seq, N_HEADS, HEAD_DIM))
        vp.append(v_f.reshape(batch, seq, N_HEADS, HEAD_DIM))
        ikp.append(ik_f[:, :IDX_DIM].reshape(batch, seq, IDX_DIM))
        plp.append(h3[:, seq - POOL_BUF:, SEG_UP * W_SEG:(SEG_UP + 1) * W_SEG])
        xp = _merge(xp, pg.reshape(mp, W_SEG), ag, h, h_tail, wpo, wao, wo, ln_g[l], ln_b[l], 256, alpha)

        hs, hs_tail = _project(xs, w_in_t, l, ms, 1024)
        k_f, v_f, qb, _, _, iqb, ik_f, _, _ = _rope(hs, tab_s, ms, 1)
        hs3 = hs.reshape(dec_batch, SAMPLE_ROWS, hs.shape[1])
        hist = jnp.pad(state_pool[l], ((0, 0), (HIST_ROWS - POOL_BUF, 0), (0, 0)))
        pg = _pool(hs3, hist, wmix, scale, SAMPLE_ROWS, past)
        iqr = iqb.reshape(dec_batch, SAMPLE_ROWS * N_IDX_HEADS, IDX_DIM)
        wcol = ik_f[:, IDX_DIM:IDX_DIM + N_IDX_HEADS].reshape(dec_batch, SAMPLE_ROWS * N_IDX_HEADS, 1)
        pad_page = lambda a: jnp.pad(a, ((0, 0), (0, PAGE_SIZE - SAMPLE_ROWS)) + ((0, 0),) * (a.ndim - 2))
        ik_new = jnp.swapaxes(pad_page(ik_f[:, :IDX_DIM].reshape(dec_batch, SAMPLE_ROWS, IDX_DIM)), 1, 2)
        sc_past = _sample_scores(page_table, iqr, wcol, cache_ik_t, l,
                                 n_table_pages // SCORE_PAGES_PER_STEP, SCORE_PAGES_PER_STEP)
        sc_new = _sample_scores(page_table, iqr, wcol, ik_new, None, 1, 1)
        bias_past, bias_new = _sample_select(
            sc_past[:, :dec_seq].reshape(dec_batch * dec_seq, past),
            sc_new[:, :dec_seq].reshape(dec_batch * dec_seq, PAGE_SIZE), k_top_s, dec_seq)
        pad_rows = lambda a, w: jnp.pad(a.reshape(dec_batch, dec_seq, w),
                                        ((0, 0), (0, SAMPLE_ROWS - dec_seq), (0, 0)))
        new_page = lambda a: pad_page(a.reshape(dec_batch, SAMPLE_ROWS, N_HEADS, HEAD_DIM)).reshape(
            dec_batch, PAGE_SIZE * N_HEADS, HEAD_DIM)
        ag = _sample_attention(page_table, qb, pad_rows(bias_past, past), new_page(k_f), new_page(v_f),
                               pad_rows(bias_new, PAGE_SIZE), hs, cache_k2, cache_v2, l)
        kss.append(real(k_f, N_HEADS, HEAD_DIM))
        vss.append(real(v_f, N_HEADS, HEAD_DIM))
        iks.append(real(ik_f, LANES)[:, :, :IDX_DIM])
        us = hs3[:, :dec_seq, SEG_UP * W_SEG:(SEG_UP + 1) * W_SEG]
        pls.append(jnp.concatenate([state_pool[l], us], axis=1)[:, -POOL_BUF:])
        xs = _merge(xs, pg.reshape(ms, W_SEG), ag, hs, hs_tail, wpo, wao, wo, ln_g[l], ln_b[l], ms, alpha)

    y_sample = xs.reshape(dec_batch, SAMPLE_ROWS, d_model)[:, :dec_seq]
    return (xp.reshape(batch, seq, d_model), y_sample, jnp.stack(kp), jnp.stack(vp), jnp.stack(ikp),
            jnp.stack(plp), jnp.stack(kss), jnp.stack(vss), jnp.stack(iks), jnp.stack(pls))
```
